```python
import jax, jax.numpy as jnp
from jax import lax
import numpy as np

D_MODEL = 1024
BATCH = 8
SEQ = 4096
DEPTH = 4
DEC_BATCH = 1
DEC_SEQ = 16384
PAST_LEN = 128

N_MIXERS = 3
D_FF = 4 * D_MODEL
SHORT_CONV_W = 3
CONFORMER_CONV_W = 31
FNET_GROUPS = 4
FNET_GROUP_DIM = D_MODEL // FNET_GROUPS
N_ADA = 6
EPS = 1e-6
N_LAYERS_A = (DEPTH + 2) // 3
N_LAYERS_B = (DEPTH + 1) // 3
N_LAYERS_C = DEPTH // 3

kernel_name = "hybrid_conv_fourier_conformer_adaln_encoder"


def rmsnorm(x, g):
    xf = x.astype(jnp.float32)
    y = xf * lax.rsqrt(jnp.mean(xf * xf, axis=-1, keepdims=True) + EPS)
    return (y * g.astype(jnp.float32)).astype(x.dtype)


def layernorm(x, g, b):
    xf = x.astype(jnp.float32)
    mu = jnp.mean(xf, axis=-1, keepdims=True)
    xc = xf - mu
    y = xc * lax.rsqrt(jnp.mean(xc * xc, axis=-1, keepdims=True) + EPS)
    return (y * g.astype(jnp.float32) + b.astype(jnp.float32)).astype(x.dtype)


def depthwise_conv(x, w):
    k = w.shape[0]
    pad = k // 2
    return lax.conv_general_dilated(
        x, w[:, None, :].astype(x.dtype), window_strides=(1,), padding=((pad, pad),),
        dimension_numbers=("NWC", "WIO", "NWC"), feature_group_count=x.shape[-1])


def mixer_short_conv(h, w_in, conv_w, w_out):
    b_gate, c_gate, v = jnp.split(h @ w_in, 3, axis=-1)
    return (b_gate * depthwise_conv(c_gate * v, conv_w)) @ w_out


def mixer_fourier(h, w_out, b_out):
    bsz, s, d = h.shape
    hg = h.astype(jnp.float32).reshape(bsz, s, FNET_GROUPS, FNET_GROUP_DIM)
    f = jnp.fft.fft2(hg, axes=(1, 3), norm="ortho").real
    return f.reshape(bsz, s, d).astype(h.dtype) @ w_out + b_out


def mixer_conformer(h, w_pw1, b_pw1, dw_w, dw_b, ln_g, ln_b, w_pw2, b_pw2):
    a, g = jnp.split(h @ w_pw1 + b_pw1, 2, axis=-1)
    u = a * jax.nn.sigmoid(g)
    u = depthwise_conv(u, dw_w) + dw_b
    u = jax.nn.silu(layernorm(u, ln_g, ln_b))
    return u @ w_pw2 + b_pw2


def squared_relu_mlp(h, w_up, w_down):
    return jnp.square(jax.nn.relu(h @ w_up)) @ w_down


def trunk(x, c, ada_w, ada_b, norm_mix, norm_mlp,
          a_w_in, a_conv_w, a_w_out,
          b_w_out, b_b_out,
          c_w_pw1, c_b_pw1, c_dw_w, c_dw_b, c_ln_g, c_ln_b, c_w_pw2, c_b_pw2,
          mlp_w_up, mlp_w_down, final_norm):
    c_act = jax.nn.silu(c)
    for i in range(DEPTH):
        mod = (c_act @ ada_w[i] + ada_b[i])[:, None, :]
        sh1, sc1, g1, sh2, sc2, g2 = jnp.split(mod, N_ADA, axis=-1)
        h = rmsnorm(x, norm_mix[i]) * (1 + sc1) + sh1
        kind, j = i % N_MIXERS, i // N_MIXERS
        if kind == 0:
            m = mixer_short_conv(h, a_w_in[j], a_conv_w[j], a_w_out[j])
        elif kind == 1:
            m = mixer_fourier(h, b_w_out[j], b_b_out[j])
        else:
            m = mixer_conformer(h, c_w_pw1[j], c_b_pw1[j], c_dw_w[j], c_dw_b[j],
                                c_ln_g[j], c_ln_b[j], c_w_pw2[j], c_b_pw2[j])
        x = x + g1 * m
        h = rmsnorm(x, norm_mlp[i]) * (1 + sc2) + sh2
        x = x + g2 * squared_relu_mlp(h, mlp_w_up[i], mlp_w_down[i])
    return rmsnorm(x, final_norm)


def setup_inputs(seed: int = 0) -> dict:
    key = jax.random.key(seed)
    ks = jax.random.split(key, 32)
    f32 = jnp.float32
    D = D_MODEL

    def nrm(k, shape, scale):
        return jax.random.normal(k, shape, f32) * scale

    def gain(k, shape):
        return 1.0 + 0.02 * jax.random.normal(k, shape, f32)

    return {
        "x_prompt": nrm(ks[0], (BATCH, SEQ, D), 1.0),
        "x_sample": nrm(ks[1], (DEC_BATCH, DEC_SEQ, D), 1.0),
        "c_prompt": nrm(ks[2], (BATCH, D), 1.0),
        "c_sample": nrm(ks[3], (DEC_BATCH, D), 1.0),
        "ada_w": nrm(ks[4], (DEPTH, D, N_ADA * D), 0.5 * D ** -0.5),
        "ada_b": nrm(ks[5], (DEPTH, N_ADA * D), 0.02),
        "norm_mix": gain(ks[6], (DEPTH, D)),
        "norm_mlp": gain(ks[7], (DEPTH, D)),
        "a_w_in": nrm(ks[8], (N_LAYERS_A, D, 3 * D), D ** -0.5),
        "a_conv_w": nrm(ks[9], (N_LAYERS_A, SHORT_CONV_W, D), SHORT_CONV_W ** -0.5),
        "a_w_out": nrm(ks[10], (N_LAYERS_A, D, D), D ** -0.5),
        "b_w_out": nrm(ks[11], (N_LAYERS_B, D, D), D ** -0.5),
        "b_b_out": nrm(ks[12], (N_LAYERS_B, D), 0.02),
        "c_w_pw1": nrm(ks[13], (N_LAYERS_C, D, 2 * D), D ** -0.5),
        "c_b_pw1": nrm(ks[14], (N_LAYERS_C, 2 * D), 0.02),
        "c_dw_w": nrm(ks[15], (N_LAYERS_C, CONFORMER_CONV_W, D), CONFORMER_CONV_W ** -0.5),
        "c_dw_b": nrm(ks[16], (N_LAYERS_C, D), 0.02),
        "c_ln_g": gain(ks[17], (N_LAYERS_C, D)),
        "c_ln_b": nrm(ks[18], (N_LAYERS_C, D), 0.02),
        "c_w_pw2": nrm(ks[19], (N_LAYERS_C, D, D), D ** -0.5),
        "c_b_pw2": nrm(ks[20], (N_LAYERS_C, D), 0.02),
        "mlp_w_up": nrm(ks[21], (DEPTH, D, D_FF), D ** -0.5),
        "mlp_w_down": nrm(ks[22], (DEPTH, D_FF, D), D_FF ** -0.5),
        "final_norm": gain(ks[23], (D,)),
    }


def reference(x_prompt, x_sample, c_prompt, c_sample, ada_w, ada_b, norm_mix, norm_mlp,
              a_w_in, a_conv_w, a_w_out, b_w_out, b_b_out,
              c_w_pw1, c_b_pw1, c_dw_w, c_dw_b, c_ln_g, c_ln_b, c_w_pw2, c_b_pw2,
              mlp_w_up, mlp_w_down, final_norm):
    y_prompt = trunk(x_prompt, c_prompt, ada_w, ada_b, norm_mix, norm_mlp,
                     a_w_in, a_conv_w, a_w_out, b_w_out, b_b_out,
                     c_w_pw1, c_b_pw1, c_dw_w, c_dw_b, c_ln_g, c_ln_b, c_w_pw2, c_b_pw2,
                     mlp_w_up, mlp_w_down, final_norm)
    y_sample = trunk(x_sample, c_sample, ada_w, ada_b, norm_mix, norm_mlp,
                     a_w_in, a_conv_w, a_w_out, b_w_out, b_b_out,
                     c_w_pw1, c_b_pw1, c_dw_w, c_dw_b, c_ln_g, c_ln_b, c_w_pw2, c_b_pw2,
                     mlp_w_up, mlp_w_down, final_norm)
    return (y_prompt, y_sample)
```

```python
import functools
import math

import jax
import jax.numpy as jnp
from jax import lax
from jax.experimental import pallas as pl
from jax.experimental.pallas import tpu as pltpu

EPS = 1e-6
N_ADA = 6
N_MIXERS = 3
FNET_GROUPS = 4

SUBLANES = 8
BF16_ROWS = 16
VMEM_LIMIT_BYTES = 56 * 1024 * 1024

F32 = jnp.float32
BF16 = jnp.bfloat16


def _params(n_axes):
    return pltpu.CompilerParams(
        dimension_semantics=("arbitrary",) * n_axes, vmem_limit_bytes=VMEM_LIMIT_BYTES)


def _resident(shape):
    zeros = (0,) * len(shape)
    return pl.BlockSpec(shape, lambda *_: zeros, pipeline_mode=pl.Buffered(1))


def _rms_mod(x, gain, scale, shift):
    ms = jnp.mean(x * x, axis=-1, keepdims=True)
    return (x * lax.rsqrt(ms + EPS)) * gain * (1.0 + scale) + shift


def _dot(a, b):
    return jnp.dot(a, b, preferred_element_type=F32)


def _ada_kernel(c_ref, w_ref, b_ref, o_ref):
    c = c_ref[...]
    c_act = (c * jax.nn.sigmoid(c)).astype(BF16)
    o_ref[...] = _dot(c_act, w_ref[...].astype(BF16)) + b_ref[...]


def _ada_modulation(c_all, ada_w, ada_b, col_block=1536):
    depth, d, n = ada_w.shape
    rows = c_all.shape[0]
    assert n % col_block == 0
    return pl.pallas_call(
        _ada_kernel,
        out_shape=jax.ShapeDtypeStruct((depth, rows, n), F32),
        grid=(depth, n // col_block),
        in_specs=[
            pl.BlockSpec((rows, d), lambda l, j: (0, 0)),
            pl.BlockSpec((None, d, col_block), lambda l, j: (l, 0, j)),
            pl.BlockSpec((None, 1, col_block), lambda l, j: (l, 0, j)),
        ],
        out_specs=pl.BlockSpec((None, rows, col_block), lambda l, j: (l, 0, j)),
        compiler_params=_params(2),
        name="ada_modulation",
    )(c_all, ada_w, ada_b.reshape(depth, 1, n))


def _mlp_kernel(x_ref, mod_ref, g_ref, wu_ref, wd_ref, fn_ref, o_ref, *, ff_chunk, final):
    x = x_ref[...]
    d = x.shape[-1]
    shift, scale, gate = (mod_ref[:, k * d:(k + 1) * d] for k in (3, 4, 5))
    h = _rms_mod(x, g_ref[...], scale, shift).astype(BF16)
    acc = jnp.zeros_like(x)
    for j in range(wu_ref.shape[1] // ff_chunk):
        u = jnp.maximum(_dot(h, wu_ref[:, j * ff_chunk:(j + 1) * ff_chunk]), 0.0)
        acc = acc + _dot((u * u).astype(BF16), wd_ref[j * ff_chunk:(j + 1) * ff_chunk, :])
    y = x + gate * acc
    if final:
        y = y * lax.rsqrt(jnp.mean(y * y, axis=-1, keepdims=True) + EPS) * fn_ref[...]
    o_ref[...] = y


def _mlp_layer(x, mod, gain, w_up, w_down, final_gain, *, final, tm, ff_chunk):
    bt, s, d = x.shape
    f = w_up.shape[1]
    assert s % tm == 0 and f % ff_chunk == 0
    tiles = s // tm
    xf = x.reshape(bt * s, d)
    out = pl.pallas_call(
        functools.partial(_mlp_kernel, ff_chunk=ff_chunk, final=final),
        out_shape=jax.ShapeDtypeStruct(xf.shape, F32),
        grid=(bt * tiles,),
        in_specs=[
            pl.BlockSpec((tm, d), lambda i: (i, 0)),
            pl.BlockSpec((None, 1, N_ADA * d), lambda i: (i // tiles, 0, 0)),
            _resident((1, d)),
            _resident((d, f)),
            _resident((f, d)),
            _resident((1, d)),
        ],
        out_specs=pl.BlockSpec((tm, d), lambda i: (i, 0)),
        compiler_params=_params(1),
        name="mlp_final" if final else "mlp",
    )(xf, mod, gain, w_up, w_down, final_gain)
    return out.reshape(bt, s, d)


HALO = BF16_ROWS


def _halo_specs(tm, d, n_rows):
    per = tm // HALO
    last = n_rows // HALO - 1
    return [
        pl.BlockSpec((HALO, d), lambda i: (jnp.maximum(i * per - 1, 0), 0)),
        pl.BlockSpec((tm, d), lambda i: (i, 0)),
        pl.BlockSpec((HALO, d), lambda i: (jnp.minimum((i + 1) * per, last), 0)),
    ]


def _fill_hext(hext, xp_ref, x_ref, xn_ref, gain, scale, shift, tm):
    hext[0:HALO, :] = _rms_mod(xp_ref[...], gain, scale, shift).astype(BF16)
    hext[HALO:HALO + tm, :] = _rms_mod(x_ref[...], gain, scale, shift).astype(BF16)
    hext[HALO + tm:, :] = _rms_mod(xn_ref[...], gain, scale, shift).astype(BF16)


def _zero_outside_sequence(u, tm, tiles):
    t = pl.program_id(0) % tiles
    row = lax.broadcasted_iota(jnp.int32, u.shape, 0)
    outside = ((t == 0) & (row < HALO)) | ((t == tiles - 1) & (row >= HALO + tm))
    return jnp.where(outside, 0.0, u)


def _sconv_kernel(xp_ref, x_ref, xn_ref, mod_ref, g_ref, win_ref, cw_ref, wout_ref, o_ref, hext,
                  *, tm, tiles):
    d = x_ref.shape[-1]
    n = tm + 2 * HALO
    shift, scale, gate = (mod_ref[:, k * d:(k + 1) * d] for k in (0, 1, 2))
    _fill_hext(hext, xp_ref, x_ref, xn_ref, g_ref[...], scale, shift, tm)
    cv = _dot(hext[...], win_ref[:, d:3 * d])
    u = _zero_outside_sequence(cv[:, :d] * cv[:, d:], tm, tiles)
    prev = pltpu.roll(u, 1, axis=0)[HALO:HALO + tm]
    nxt = pltpu.roll(u, n - 1, axis=0)[HALO:HALO + tm]
    y = cw_ref[0:1, :] * prev + cw_ref[1:2, :] * u[HALO:HALO + tm] + cw_ref[2:3, :] * nxt
    b = _dot(hext[HALO:HALO + tm, :], win_ref[:, 0:d])
    m = _dot((b * y).astype(BF16), wout_ref[...])
    o_ref[...] = x_ref[...] + gate * m


def _sconv_layer(x, mod, gain, w_in, conv_w, w_out, *, tm):
    bt, s, d = x.shape
    assert s % tm == 0 and tm % HALO == 0 and conv_w.shape[0] == 3
    tiles = s // tm
    xf = x.reshape(bt * s, d)
    out = pl.pallas_call(
        functools.partial(_sconv_kernel, tm=tm, tiles=tiles),
        out_shape=jax.ShapeDtypeStruct(xf.shape, F32),
        grid=(bt * tiles,),
        in_specs=_halo_specs(tm, d, bt * s) + [
            pl.BlockSpec((None, 1, N_ADA * d), lambda i: (i // tiles, 0, 0)),
            _resident((1, d)),
            _resident((d, 3 * d)),
            _resident((3, d)),
            _resident((d, d)),
        ],
        out_specs=pl.BlockSpec((tm, d), lambda i: (i, 0)),
        scratch_shapes=[pltpu.VMEM((tm + 2 * HALO, d), BF16)],
        compiler_params=_params(1),
        name="short_conv",
    )(xf, xf, xf, mod, gain, w_in, conv_w, w_out)
    return out.reshape(bt, s, d)


CONV_ROWS = 32
CONV_LANES = 512


def _conformer_kernel(xp_ref, x_ref, xn_ref, mod_ref, g_ref, w1_ref, b1_ref, dw_ref, dwb_ref,
                      lng_ref, lnb_ref, w2_ref, b2_ref, o_ref, hext, shifted, conv,
                      *, tm, tiles, width):
    d = x_ref.shape[-1]
    n = tm + 2 * HALO
    pad = width // 2
    shift, scale, gate = (mod_ref[:, k * d:(k + 1) * d] for k in (0, 1, 2))
    _fill_hext(hext, xp_ref, x_ref, xn_ref, g_ref[...], scale, shift, tm)
    ag = _dot(hext[...], w1_ref[...]) + b1_ref[...]
    u = _zero_outside_sequence(ag[:, :d] * jax.nn.sigmoid(ag[:, d:]), tm, tiles)
    shifted[0] = u
    for p in range(1, SUBLANES):
        shifted[p] = pltpu.roll(u, n - p, axis=0)

    def chunk(ci, carry):
        base = pl.multiple_of(ci * CONV_ROWS, CONV_ROWS)
        for l0 in range(0, d, CONV_LANES):
            acc = jnp.zeros((CONV_ROWS, CONV_LANES), F32)
            for k in range(width):
                off = HALO - pad + k
                win = shifted[off % SUBLANES,
                              pl.ds(base + (off // SUBLANES) * SUBLANES, CONV_ROWS),
                              l0:l0 + CONV_LANES]
                acc = acc + dw_ref[k:k + 1, l0:l0 + CONV_LANES] * win
            conv[pl.ds(base, CONV_ROWS), l0:l0 + CONV_LANES] = acc + dwb_ref[:, l0:l0 + CONV_LANES]
        return carry

    lax.fori_loop(0, tm // CONV_ROWS, chunk, 0)
    v = conv[...]
    mu = jnp.mean(v, axis=-1, keepdims=True)
    vc = v - mu
    ln = vc * lax.rsqrt(jnp.mean(vc * vc, axis=-1, keepdims=True) + EPS) * lng_ref[...] + lnb_ref[...]
    act = (ln * jax.nn.sigmoid(ln)).astype(BF16)
    m = _dot(act, w2_ref[...]) + b2_ref[...]
    o_ref[...] = x_ref[...] + gate * m


def _conformer_layer(x, mod, gain, w_pw1, b_pw1, dw_w, dw_b, ln_g, ln_b, w_pw2, b_pw2, *, tm):
    bt, s, d = x.shape
    width = dw_w.shape[0]
    assert s % tm == 0 and tm % CONV_ROWS == 0 and width // 2 < HALO and d % CONV_LANES == 0
    tiles = s // tm
    n = tm + 2 * HALO
    xf = x.reshape(bt * s, d)
    out = pl.pallas_call(
        functools.partial(_conformer_kernel, tm=tm, tiles=tiles, width=width),
        out_shape=jax.ShapeDtypeStruct(xf.shape, F32),
        grid=(bt * tiles,),
        in_specs=_halo_specs(tm, d, bt * s) + [
            pl.BlockSpec((None, 1, N_ADA * d), lambda i: (i // tiles, 0, 0)),
            _resident((1, d)),
            _resident((d, 2 * d)),
            _resident((1, 2 * d)),
            _resident((width, d)),
            _resident((1, d)),
            _resident((1, d)),
            _resident((1, d)),
            _resident((d, d)),
            _resident((1, d)),
        ],
        out_specs=pl.BlockSpec((tm, d), lambda i: (i, 0)),
        scratch_shapes=[
            pltpu.VMEM((n, d), BF16),
            pltpu.VMEM((SUBLANES, n, d), F32),
            pltpu.VMEM((tm, d), F32),
        ],
        compiler_params=_params(1),
        name="conformer",
    )(xf, xf, xf, mod, gain, w_pw1, b_pw1, dw_w, dw_b, ln_g, ln_b, w_pw2, b_pw2)
    return out.reshape(bt, s, d)


def _fft_factors(s):
    known = {4096: (16, 16, 16), 16384: (32, 32, 16)}
    if s in known:
        return known[s]
    best = None
    for a in (8, 16, 32, 64):
        for c in (8, 16, 32, 64):
            if s % (a * c) == 0:
                b = s // (a * c)
                cost = max(a, 32) + 2 * max(b, 16) + 2 * max(c, 16)
                if best is None or cost < best[0]:
                    best = (cost, (a, b, c))
    assert best is not None, f"sequence length {s} needs two factors that are multiples of 8"
    return best[1]


def _cos_sin(phase, period):
    ang = phase.astype(F32) * F32(2.0 * math.pi / period)
    return jnp.cos(ang), jnp.sin(ang)


def _fft_tables(s, dg):
    a_n, b_n, c_n = _fft_factors(s)
    i32 = jnp.int32
    eye = jnp.eye(SUBLANES, dtype=F32)
    ar = functools.partial(jnp.arange, dtype=i32)

    ka = (ar(a_n // 8)[:, None] * 8 + ar(8)[None, :])
    ph = (ka[None, :, :, None] * (ar(a_n)[None, None, None, :] * b_n + ar(b_n)[:, None, None, None])) % (a_n * b_n)
    co, si = _cos_sin(ph, a_n * b_n)
    def s1(m):
        return jnp.einsum("bhka,cq->bhckaq", m, eye)
    m1 = jnp.stack([s1(co), s1(-si)], axis=1).reshape(b_n, 2 * a_n * 8, a_n * 8).astype(BF16)

    kb = ar(b_n)
    kk = ka[:, None, None, :, None] + a_n * kb[None, None, :, None, None]
    ph = (kb[None, None, :, None, None] * ar(b_n)[None, None, None, None, :] * (a_n * c_n)
          + ar(c_n)[None, :, None, None, None] * kk) % s
    co, si = _cos_sin(ph, s)
    def s2(m):
        return jnp.einsum("hckjb,jl->hckjbl", m, eye)
    top = jnp.stack([s2(co), s2(si)], axis=4)
    bot = jnp.stack([s2(-si), s2(co)], axis=4)
    m2 = jnp.stack([top, bot], axis=2).reshape(a_n // 8, c_n, 2 * b_n * 8, 2 * b_n * 8).astype(BF16)

    ph = (ar(c_n)[:, None] * ar(c_n)[None, :]) % c_n
    co, si = _cos_sin(ph, c_n)
    def s3(m):
        return jnp.einsum("kc,jl->kjcl", m, eye)
    top = jnp.stack([s3(co), s3(si)], axis=2)
    bot = jnp.stack([s3(-si), s3(co)], axis=2)
    m3 = jnp.stack([top, bot], axis=0).reshape(2 * c_n * 8, 2 * c_n * 8).astype(BF16)

    ph = (ar(dg)[:, None] * ar(dg)[None, :]) % dg
    co, si = _cos_sin(ph, dg)
    norm = F32(1.0 / math.sqrt(s * dg))
    return m1, m2, m3, (co * norm).astype(BF16), (si * norm).astype(BF16)


def _fft1_kernel(x_ref, mod_ref, g_ref, m_ref, or_ref, oi_ref):
    d = x_ref.shape[-1]
    rows = x_ref.shape[0] * x_ref.shape[1]
    shift, scale = (mod_ref[:, k * d:(k + 1) * d] for k in (0, 1))
    h = _rms_mod(x_ref[...].reshape(rows, d), g_ref[...], scale, shift).astype(BF16)
    y = _dot(m_ref[...], h)
    or_ref[...] = y[:rows].reshape(or_ref.shape)
    oi_ref[...] = y[rows:].reshape(oi_ref.shape)


def _fft2_kernel(tr_ref, ti_ref, m_ref, or_ref, oi_ref):
    d = tr_ref.shape[-1]
    rows = tr_ref.shape[0] * tr_ref.shape[1]
    t = jnp.concatenate([tr_ref[...].reshape(rows, d), ti_ref[...].reshape(rows, d)], axis=0).astype(BF16)
    y = _dot(m_ref[...], t)
    or_ref[...] = y[:rows].reshape(or_ref.shape)
    oi_ref[...] = y[rows:].reshape(oi_ref.shape)


def _fft3_kernel(tr_ref, ti_ref, x_ref, mod_ref, m_ref, cc_ref, sc_ref, w_ref, b_ref, o_ref, yr, yi,
                 *, groups):
    kb_n, ah_n, c_n = tr_ref.shape[0], tr_ref.shape[1], tr_ref.shape[2]
    d = tr_ref.shape[-1]
    rows = c_n * SUBLANES
    gate = mod_ref[:, 2 * d:3 * d]
    for j in range(kb_n):
        for h in range(ah_n):
            t = jnp.concatenate([tr_ref[j, h].reshape(rows, d), ti_ref[j, h].reshape(rows, d)],
                                axis=0).astype(BF16)
            y = _dot(m_ref[...], t).astype(BF16)
            r0 = (j * ah_n + h) * rows
            yr[r0:r0 + rows, :] = y[:rows]
            yi[r0:r0 + rows, :] = y[rows:]
    dg = d // groups
    f = jnp.concatenate(
        [_dot(yr[:, g * dg:(g + 1) * dg], cc_ref[...]) + _dot(yi[:, g * dg:(g + 1) * dg], sc_ref[...])
         for g in range(groups)], axis=1)
    m = _dot(f.astype(BF16), w_ref[...]) + b_ref[...]
    for j in range(kb_n):
        for h in range(ah_n):
            r0 = (j * ah_n + h) * rows
            o_ref[:, j, h] = x_ref[:, j, h] + (gate * m[r0:r0 + rows]).reshape(c_n, SUBLANES, d)


def _fourier_layer(x, mod, gain, w_out, b_out, *, rows3=512):
    bt, s, d = x.shape
    a_n, b_n, c_n = _fft_factors(s)
    assert a_n * b_n * c_n == s and a_n % 8 == 0 and c_n % 8 == 0 and d % FNET_GROUPS == 0
    ah_n, ch_n = a_n // 8, c_n // 8
    m1, m2, m3, cc, sc = _fft_tables(s, d // FNET_GROUPS)
    mod_spec = lambda n_axes: pl.BlockSpec((None, 1, N_ADA * d), lambda *i: (i[0], 0, 0))

    x1 = x.reshape(bt, a_n, b_n * ch_n, 8, d)
    t1_shape = jax.ShapeDtypeStruct((bt, ah_n, ch_n, 8, b_n, 8, d), F32)
    t1_spec = pl.BlockSpec((None, ah_n, None, 8, None, 8, d),
                           lambda i, q: (i, 0, q % ch_n, 0, q // ch_n, 0, 0))
    t1r, t1i = pl.pallas_call(
        _fft1_kernel,
        out_shape=(t1_shape, t1_shape),
        grid=(bt, b_n * ch_n),
        in_specs=[
            pl.BlockSpec((None, a_n, None, 8, d), lambda i, q: (i, 0, q, 0, 0)),
            mod_spec(2),
            _resident((1, d)),
            pl.BlockSpec((None, 2 * a_n * 8, a_n * 8), lambda i, q: (q // ch_n, 0, 0)),
        ],
        out_specs=(t1_spec, t1_spec),
        compiler_params=_params(2),
        name="fourier_seq1",
    )(x1, mod, gain, m1)

    t1r = t1r.reshape(bt, ah_n, c_n, b_n, 8, d)
    t1i = t1i.reshape(bt, ah_n, c_n, b_n, 8, d)
    t1_in = pl.BlockSpec((None, None, None, b_n, 8, d), lambda i, h, c: (i, h, c, 0, 0, 0))
    t2_shape = jax.ShapeDtypeStruct((bt, b_n, ah_n, c_n, 8, d), F32)
    t2_spec = pl.BlockSpec((None, b_n, None, None, 8, d), lambda i, h, c: (i, 0, h, c, 0, 0))
    t2r, t2i = pl.pallas_call(
        _fft2_kernel,
        out_shape=(t2_shape, t2_shape),
        grid=(bt, ah_n, c_n),
        in_specs=[
            t1_in, t1_in,
            pl.BlockSpec((None, None, 2 * b_n * 8, 2 * b_n * 8), lambda i, h, c: (h, c, 0, 0)),
        ],
        out_specs=(t2_spec, t2_spec),
        compiler_params=_params(3),
        name="fourier_seq2",
    )(t1r, t1i, m2)

    kb_blk = max(1, min(b_n, rows3 // (ah_n * c_n * 8)))
    while b_n % kb_blk:
        kb_blk -= 1
    rows = kb_blk * ah_n * c_n * 8
    t2_in = pl.BlockSpec((None, kb_blk, ah_n, c_n, 8, d), lambda i, k: (i, k, 0, 0, 0, 0))
    xo_spec = pl.BlockSpec((None, c_n, kb_blk, ah_n, 8, d), lambda i, k: (i, 0, k, 0, 0, 0))
    dg = d // FNET_GROUPS
    out = pl.pallas_call(
        functools.partial(_fft3_kernel, groups=FNET_GROUPS),
        out_shape=jax.ShapeDtypeStruct((bt, c_n, b_n, ah_n, 8, d), F32),
        grid=(bt, b_n // kb_blk),
        in_specs=[
            t2_in, t2_in, xo_spec, mod_spec(2),
            _resident((2 * c_n * 8, 2 * c_n * 8)),
            _resident((dg, dg)),
            _resident((dg, dg)),
            _resident((d, d)),
            _resident((1, d)),
        ],
        out_specs=xo_spec,
        scratch_shapes=[pltpu.VMEM((rows, d), BF16), pltpu.VMEM((rows, d), BF16)],
        compiler_params=_params(2),
        name="fourier_seq3",
    )(t2r, t2i, x.reshape(bt, c_n, b_n, ah_n, 8, d), mod, m3, cc, sc, w_out, b_out)
    return out.reshape(bt, s, d)


def _tile_rows(s, target):
    tm = min(s, target)
    while s % tm:
        tm //= 2
    return tm


def _trunk(x, mods, w, *, mlp_tm=512, conv_tm=512, ff_chunk=1024):
    depth = w["norm_mix"].shape[0]
    s = x.shape[1]
    row = lambda v: v.reshape(1, -1)
    for i in range(depth):
        mod = mods[i]
        kind, j = i % N_MIXERS, i // N_MIXERS
        gain = row(w["norm_mix"][i])
        if kind == 0:
            x = _sconv_layer(x, mod, gain, w["a_w_in"][j], w["a_conv_w"][j], w["a_w_out"][j],
                             tm=_tile_rows(s, conv_tm))
        elif kind == 1:
            x = _fourier_layer(x, mod, gain, w["b_w_out"][j], row(w["b_b_out"][j]))
        else:
            x = _conformer_layer(x, mod, gain, w["c_w_pw1"][j], row(w["c_b_pw1"][j]), w["c_dw_w"][j],
                                 row(w["c_dw_b"][j]), row(w["c_ln_g"][j]), row(w["c_ln_b"][j]),
                                 w["c_w_pw2"][j], row(w["c_b_pw2"][j]), tm=_tile_rows(s, conv_tm))
        x = _mlp_layer(x, mod, row(w["norm_mlp"][i]), w["mlp_w_up"][i], w["mlp_w_down"][i],
                       row(w["final_norm"]), final=(i == depth - 1), tm=_tile_rows(s, mlp_tm),
                       ff_chunk=min(ff_chunk, w["mlp_w_up"].shape[-1]))
    return x


def kernel(x_prompt, x_sample, c_prompt, c_sample, ada_w, ada_b, norm_mix, norm_mlp, a_w_in, a_conv_w, a_w_out, b_w_out, b_b_out, c_w_pw1, c_b_pw1, c_dw_w, c_dw_b, c_ln_g, c_ln_b, c_w_pw2, c_b_pw2, mlp_w_up, mlp_w_down, final_norm):
    d = x_prompt.shape[-1]
    n_p, n_s = c_prompt.shape[0], c_sample.shape[0]
    rows = -(-(n_p + n_s) // SUBLANES) * SUBLANES
    c_all = jnp.zeros((rows, d), F32).at[:n_p].set(c_prompt).at[n_p:n_p + n_s].set(c_sample)
    mod = _ada_modulation(c_all, ada_w, ada_b)
    mods_p = mod[:, :n_p, None, :]
    mods_s = mod[:, n_p:n_p + n_s, None, :]
    w = dict(
        norm_mix=norm_mix, norm_mlp=norm_mlp, final_norm=final_norm,
        a_w_in=a_w_in.astype(BF16), a_conv_w=a_conv_w, a_w_out=a_w_out.astype(BF16),
        b_w_out=b_w_out.astype(BF16), b_b_out=b_b_out,
        c_w_pw1=c_w_pw1.astype(BF16), c_b_pw1=c_b_pw1, c_dw_w=c_dw_w, c_dw_b=c_dw_b,
        c_ln_g=c_ln_g, c_ln_b=c_ln_b, c_w_pw2=c_w_pw2.astype(BF16), c_b_pw2=c_b_pw2,
        mlp_w_up=mlp_w_up.astype(BF16), mlp_w_down=mlp_w_down.astype(BF16),
    )
    return _trunk(x_prompt, mods_p, w), _trunk(x_sample, mods_s, w)
```

```python
import functools
import math

import jax
import jax.numpy as jnp
from jax import lax
from jax.experimental import pallas as pl
from jax.experimental.pallas import tpu as pltpu

EPS = 1e-6
N_ADA = 6
N_MIXERS = 3
FNET_GROUPS = 4

SUBLANES = 8
LANES = 128
BF16_ROWS = 16
VMEM_LIMIT_BYTES = 56 * 1024 * 1024

F32 = jnp.float32
BF16 = jnp.bfloat16
U32 = jnp.uint32


def _params(n_axes):
    return pltpu.CompilerParams(
        dimension_semantics=("arbitrary",) * n_axes, vmem_limit_bytes=VMEM_LIMIT_BYTES)


def _resident(shape):
    zeros = (0,) * len(shape)
    return pl.BlockSpec(shape, lambda *_: zeros, pipeline_mode=pl.Buffered(1))


def _layer_slice(shape, layer):
    index = (layer,) + (0,) * len(shape)
    return pl.BlockSpec((None,) + shape, lambda *_: index, pipeline_mode=pl.Buffered(1))


def _rows(p):
    return p.reshape(p.shape[0], 1, p.shape[1])


def _rms_mod(x, gain, scale, shift):
    ms = jnp.mean(x * x, axis=-1, keepdims=True)
    return (x * lax.rsqrt(ms + EPS)) * gain * (1.0 + scale) + shift


def _dot(a, b):
    return jnp.dot(a, b, preferred_element_type=F32)


def _ada_kernel(c_ref, w_ref, b_ref, o_ref):
    c = c_ref[...]
    c_act = (c * jax.nn.sigmoid(c)).astype(BF16)
    o_ref[...] = _dot(c_act, w_ref[...].astype(BF16)) + b_ref[...]


def _ada_modulation(c_all, ada_w, ada_b, col_block=1536):
    depth, d, n = ada_w.shape
    rows = c_all.shape[0]
    assert n % col_block == 0
    return pl.pallas_call(
        _ada_kernel,
        out_shape=jax.ShapeDtypeStruct((depth, rows, n), F32),
        grid=(depth, n // col_block),
        in_specs=[
            pl.BlockSpec((rows, d), lambda l, j: (0, 0)),
            pl.BlockSpec((None, d, col_block), lambda l, j: (l, 0, j)),
            pl.BlockSpec((None, 1, col_block), lambda l, j: (l, 0, j)),
        ],
        out_specs=pl.BlockSpec((None, rows, col_block), lambda l, j: (l, 0, j)),
        compiler_params=_params(2),
        name="ada_modulation",
    )(c_all, ada_w, _rows(ada_b))


def _mlp_kernel(x_ref, mod_ref, g_ref, wu_ref, wd_ref, fn_ref, o_ref, *, ff_chunk, final):
    x = x_ref[...]
    d = x.shape[-1]
    shift, scale, gate = (mod_ref[:, k * d:(k + 1) * d] for k in (3, 4, 5))
    h = _rms_mod(x, g_ref[...], scale, shift).astype(BF16)
    acc = jnp.zeros_like(x)
    for j in range(wu_ref.shape[1] // ff_chunk):
        u = jnp.maximum(_dot(h, wu_ref[:, j * ff_chunk:(j + 1) * ff_chunk]), 0.0)
        acc = acc + _dot((u * u).astype(BF16), wd_ref[j * ff_chunk:(j + 1) * ff_chunk, :])
    y = x + gate * acc
    if final:
        y = y * lax.rsqrt(jnp.mean(y * y, axis=-1, keepdims=True) + EPS) * fn_ref[...]
    o_ref[...] = y


def _mlp_layer(x, mod, w, layer, *, final, tm, ff_chunk):
    bt, s, d = x.shape
    f = w["mlp_w_up"].shape[-1]
    ff_chunk = min(ff_chunk, f)
    assert s % tm == 0 and f % ff_chunk == 0
    tiles = s // tm
    xf = x.reshape(bt * s, d)
    out = pl.pallas_call(
        functools.partial(_mlp_kernel, ff_chunk=ff_chunk, final=final),
        out_shape=jax.ShapeDtypeStruct(xf.shape, F32),
        grid=(bt * tiles,),
        in_specs=[
            pl.BlockSpec((tm, d), lambda i: (i, 0)),
            pl.BlockSpec((None, 1, N_ADA * d), lambda i: (i // tiles, 0, 0)),
            _layer_slice((1, d), layer),
            _layer_slice((d, f), layer),
            _layer_slice((f, d), layer),
            _resident((1, d)),
        ],
        out_specs=pl.BlockSpec((tm, d), lambda i: (i, 0)),
        compiler_params=_params(1),
        name="mlp_final" if final else "mlp",
    )(xf, mod, w["norm_mlp"], w["mlp_w_up"], w["mlp_w_down"], w["final_norm"])
    return out.reshape(bt, s, d)


HALO = BF16_ROWS


def _halo_specs(tm, d, n_rows):
    per = tm // HALO
    last = n_rows // HALO - 1
    return [
        pl.BlockSpec((HALO, d), lambda i: (jnp.maximum(i * per - 1, 0), 0)),
        pl.BlockSpec((tm, d), lambda i: (i, 0)),
        pl.BlockSpec((HALO, d), lambda i: (jnp.minimum((i + 1) * per, last), 0)),
    ]


def _fill_hext(hext, xp_ref, x_ref, xn_ref, gain, scale, shift, tm):
    hext[0:HALO, :] = _rms_mod(xp_ref[...], gain, scale, shift).astype(BF16)
    hext[HALO:HALO + tm, :] = _rms_mod(x_ref[...], gain, scale, shift).astype(BF16)
    hext[HALO + tm:, :] = _rms_mod(xn_ref[...], gain, scale, shift).astype(BF16)


def _zero_outside_sequence(u, tm, tiles):
    t = pl.program_id(0) % tiles
    row = lax.broadcasted_iota(jnp.int32, u.shape, 0)
    outside = ((t == 0) & (row < HALO)) | ((t == tiles - 1) & (row >= HALO + tm))
    return jnp.where(outside, 0.0, u)


def _sconv_kernel(xp_ref, x_ref, xn_ref, mod_ref, g_ref, win_ref, cw_ref, wout_ref, o_ref, hext,
                  *, tm, tiles):
    d = x_ref.shape[-1]
    n = tm + 2 * HALO
    shift, scale, gate = (mod_ref[:, k * d:(k + 1) * d] for k in (0, 1, 2))
    _fill_hext(hext, xp_ref, x_ref, xn_ref, g_ref[...], scale, shift, tm)
    cv = _dot(hext[...], win_ref[:, d:3 * d])
    u = _zero_outside_sequence(cv[:, :d] * cv[:, d:], tm, tiles)
    prev = pltpu.roll(u, 1, axis=0)[HALO:HALO + tm]
    nxt = pltpu.roll(u, n - 1, axis=0)[HALO:HALO + tm]
    y = cw_ref[0:1, :] * prev + cw_ref[1:2, :] * u[HALO:HALO + tm] + cw_ref[2:3, :] * nxt
    b = _dot(hext[HALO:HALO + tm, :], win_ref[:, 0:d])
    m = _dot((b * y).astype(BF16), wout_ref[...])
    o_ref[...] = x_ref[...] + gate * m


def _sconv_layer(x, mod, w, layer, j, *, tm):
    bt, s, d = x.shape
    assert s % tm == 0 and tm % HALO == 0 and w["a_conv_w"].shape[1] == 3
    tiles = s // tm
    xf = x.reshape(bt * s, d)
    out = pl.pallas_call(
        functools.partial(_sconv_kernel, tm=tm, tiles=tiles),
        out_shape=jax.ShapeDtypeStruct(xf.shape, F32),
        grid=(bt * tiles,),
        in_specs=_halo_specs(tm, d, bt * s) + [
            pl.BlockSpec((None, 1, N_ADA * d), lambda i: (i // tiles, 0, 0)),
            _layer_slice((1, d), layer),
            _layer_slice((d, 3 * d), j),
            _layer_slice((3, d), j),
            _layer_slice((d, d), j),
        ],
        out_specs=pl.BlockSpec((tm, d), lambda i: (i, 0)),
        scratch_shapes=[pltpu.VMEM((tm + 2 * HALO, d), BF16)],
        compiler_params=_params(1),
        name="short_conv",
    )(xf, xf, xf, mod, w["norm_mix"], w["a_w_in"], w["a_conv_w"], w["a_w_out"])
    return out.reshape(bt, s, d)


CONV_ROWS = 2 * BF16_ROWS


def _conformer_kernel(xp_ref, x_ref, xn_ref, mod_ref, g_ref, w1_ref, b1_ref, dw_ref, dwb_ref,
                      lng_ref, lnb_ref, w2_ref, b2_ref, o_ref, hext, even, odd, wtap, conv,
                      *, tm, tiles, width):
    d = x_ref.shape[-1]
    n = tm + 2 * HALO
    pad = width // 2
    lane_tiles = d // LANES

    @pl.when(pl.program_id(0) == 0)
    def _():
        for k in range(width):
            for j in range(lane_tiles):
                tap = dw_ref[k:k + 1, j * LANES:(j + 1) * LANES]
                wtap[k, j] = jnp.broadcast_to(tap, (BF16_ROWS, LANES)).astype(BF16)

    shift, scale, gate = (mod_ref[:, k * d:(k + 1) * d] for k in (0, 1, 2))
    _fill_hext(hext, xp_ref, x_ref, xn_ref, g_ref[...], scale, shift, tm)
    ag = _dot(hext[...], w1_ref[...]) + b1_ref[...]
    u = _zero_outside_sequence(ag[:, :d] * jax.nn.sigmoid(ag[:, d:]), tm, tiles)
    pe = pltpu.bitcast(u.astype(BF16), U32)
    po = pltpu.bitcast(pltpu.roll(u, n - 1, axis=0).astype(BF16), U32)
    for j in range(lane_tiles):
        even[j] = pe[:, j * LANES:(j + 1) * LANES]
        odd[j] = po[:, j * LANES:(j + 1) * LANES]

    half = BF16_ROWS // 2

    def chunk(ci, carry):
        word0 = pl.multiple_of(ci * (CONV_ROWS // 2), CONV_ROWS // 2)
        row0 = pl.multiple_of(ci * CONV_ROWS, CONV_ROWS)
        for j in range(lane_tiles):
            acc = [jnp.zeros((BF16_ROWS, LANES), F32) for _ in range(CONV_ROWS // BF16_ROWS)]
            for k in range(width):
                off = HALO - pad + k
                src = odd if off % 2 else even
                tap = wtap[k, j].astype(F32)
                for a in range(len(acc)):
                    words = src[j, pl.ds(word0 + off // 2 + a * half, half), :]
                    acc[a] = acc[a] + tap * pltpu.bitcast(words, BF16).astype(F32)
            bias = dwb_ref[:, j * LANES:(j + 1) * LANES]
            for a in range(len(acc)):
                conv[pl.ds(row0 + a * BF16_ROWS, BF16_ROWS), j * LANES:(j + 1) * LANES] = acc[a] + bias
        return carry

    lax.fori_loop(0, tm // CONV_ROWS, chunk, 0)
    v = conv[...]
    mu = jnp.mean(v, axis=-1, keepdims=True)
    vc = v - mu
    ln = vc * lax.rsqrt(jnp.mean(vc * vc, axis=-1, keepdims=True) + EPS) * lng_ref[...] + lnb_ref[...]
    act = (ln * jax.nn.sigmoid(ln)).astype(BF16)
    m = _dot(act, w2_ref[...]) + b2_ref[...]
    o_ref[...] = x_ref[...] + gate * m


def _conformer_layer(x, mod, w, layer, j, *, tm):
    bt, s, d = x.shape
    width = w["c_dw_w"].shape[1]
    assert s % tm == 0 and tm % CONV_ROWS == 0 and width // 2 < HALO and d % LANES == 0
    tiles = s // tm
    n = tm + 2 * HALO
    xf = x.reshape(bt * s, d)
    out = pl.pallas_call(
        functools.partial(_conformer_kernel, tm=tm, tiles=tiles, width=width),
        out_shape=jax.ShapeDtypeStruct(xf.shape, F32),
        grid=(bt * tiles,),
        in_specs=_halo_specs(tm, d, bt * s) + [
            pl.BlockSpec((None, 1, N_ADA * d), lambda i: (i // tiles, 0, 0)),
            _layer_slice((1, d), layer),
            _layer_slice((d, 2 * d), j),
            _layer_slice((1, 2 * d), j),
            _layer_slice((width, d), j),
            _layer_slice((1, d), j),
            _layer_slice((1, d), j),
            _layer_slice((1, d), j),
            _layer_slice((d, d), j),
            _layer_slice((1, d), j),
        ],
        out_specs=pl.BlockSpec((tm, d), lambda i: (i, 0)),
        scratch_shapes=[
            pltpu.VMEM((n, d), BF16),
            pltpu.VMEM((d // LANES, n // 2, LANES), U32),
            pltpu.VMEM((d // LANES, n // 2, LANES), U32),
            pltpu.VMEM((width, d // LANES, BF16_ROWS, LANES), BF16),
            pltpu.VMEM((tm, d), F32),
        ],
        compiler_params=_params(1),
        name="conformer",
    )(xf, xf, xf, mod, w["norm_mix"], w["c_w_pw1"], w["c_b_pw1"], w["c_dw_w"], w["c_dw_b"],
      w["c_ln_g"], w["c_ln_b"], w["c_w_pw2"], w["c_b_pw2"])
    return out.reshape(bt, s, d)


def _fft_factors(s):
    known = {4096: (16, 16, 16), 16384: (32, 32, 16)}
    if s in known:
        return known[s]
    best = None
    for a in (8, 16, 32, 64):
        for c in (8, 16, 32, 64):
            if s % (a * c) == 0:
                b = s // (a * c)
                cost = max(a, 32) + 2 * max(b, 16) + 2 * max(c, 16)
                if best is None or cost < best[0]:
                    best = (cost, (a, b, c))
    assert best is not None, f"sequence length {s} needs two factors that are multiples of 8"
    return best[1]


def _cos_sin(phase, period):
    ang = phase.astype(F32) * F32(2.0 * math.pi / period)
    return jnp.cos(ang), jnp.sin(ang)


def _kron8_dft(n):
    r = lax.broadcasted_iota(jnp.int32, (8 * n, 8 * n), 0)
    q = lax.broadcasted_iota(jnp.int32, (8 * n, 8 * n), 1)
    co, si = _cos_sin(((r // 8) * (q // 8)) % n, n)
    same = (r % 8) == (q % 8)
    return jnp.where(same, co, 0.0), jnp.where(same, si, 0.0)


def _fft_tables(s, dg):
    a_n, b_n, c_n = _fft_factors(s)
    i32 = jnp.int32
    ar = functools.partial(jnp.arange, dtype=i32)

    rows, cols = 2 * a_n * 8, a_n * 8
    r = lax.broadcasted_iota(i32, (rows, cols), 0)
    q = lax.broadcasted_iota(i32, (rows, cols), 1)
    ka = ((r // 64) % (a_n // 8)) * 8 + r % 8
    co, si = _cos_sin((ka * (q // 8)) % a_n, a_n)
    m1 = jnp.where(((r // 8) % 8) == (q % 8), jnp.where(r < a_n * 8, co, -si), 0.0).astype(BF16)

    cm2, sm2 = _kron8_dft(b_n)
    cm3, sm3 = _kron8_dft(c_n)

    ka = ar(a_n // 8)[:, None, None] * 8 + ar(8)[None, None, :]
    tw1 = _cos_sin((ar(b_n)[None, :, None] * ka) % (a_n * b_n), a_n * b_n)
    tw1 = [t.reshape(a_n // 8, 1, b_n * 8) for t in tw1]
    kk = ka[None] + a_n * ar(b_n)[:, None, None, None]
    tw2 = _cos_sin((ar(c_n)[None, None, :, None] * kk) % s, s)
    tw2 = [t.reshape(b_n, a_n // 8, 1, c_n * 8) for t in tw2]

    co, si = _cos_sin((ar(dg)[:, None] * ar(dg)[None, :]) % dg, dg)
    norm = F32(1.0 / math.sqrt(s * dg))
    return m1, (cm2, sm2), (cm3, sm3), tw1, tw2, (co * norm).astype(BF16), (si * norm).astype(BF16)


def _twiddled_matrix(cm, sm, tc, ts):
    c2 = cm * tc - sm * ts
    s2 = cm * ts + sm * tc
    top = jnp.concatenate([c2, s2], axis=1)
    bot = jnp.concatenate([-s2, c2], axis=1)
    return jnp.concatenate([top, bot], axis=0).astype(BF16)


def _fft1_kernel(x_ref, mod_ref, g_ref, m_ref, or_ref, oi_ref):
    d = x_ref.shape[-1]
    rows = x_ref.shape[0] * x_ref.shape[1]
    shift, scale = (mod_ref[:, k * d:(k + 1) * d] for k in (0, 1))
    h = _rms_mod(x_ref[...].reshape(rows, d), g_ref[...], scale, shift).astype(BF16)
    y = _dot(m_ref[...], h)
    or_ref[...] = y[:rows].reshape(or_ref.shape)
    oi_ref[...] = y[rows:].reshape(oi_ref.shape)


def _fft2_kernel(tr_ref, ti_ref, cm_ref, sm_ref, tc_ref, ts_ref, or_ref, oi_ref, mat):
    d = tr_ref.shape[-1]
    rows = tr_ref.shape[0] * tr_ref.shape[1]

    @pl.when(pl.program_id(2) == 0)
    def _():
        mat[...] = _twiddled_matrix(cm_ref[...], sm_ref[...], tc_ref[...], ts_ref[...])

    t = jnp.concatenate([tr_ref[...].reshape(rows, d), ti_ref[...].reshape(rows, d)], axis=0).astype(BF16)
    y = _dot(mat[...], t)
    or_ref[...] = y[:rows].reshape(or_ref.shape)
    oi_ref[...] = y[rows:].reshape(oi_ref.shape)


def _fft3_kernel(tr_ref, ti_ref, x_ref, mod_ref, cm_ref, sm_ref, tc_ref, ts_ref, cc_ref, sc_ref,
                 w_ref, b_ref, o_ref, yr, yi, *, groups):
    kb_n, ah_n, c_n = tr_ref.shape[0], tr_ref.shape[1], tr_ref.shape[2]
    d = tr_ref.shape[-1]
    rows = c_n * SUBLANES
    gate = mod_ref[:, 2 * d:3 * d]
    for j in range(kb_n):
        for h in range(ah_n):
            mat = _twiddled_matrix(cm_ref[...], sm_ref[...], tc_ref[j, h], ts_ref[j, h])
            t = jnp.concatenate([tr_ref[j, h].reshape(rows, d), ti_ref[j, h].reshape(rows, d)],
                                axis=0).astype(BF16)
            y = _dot(mat, t).astype(BF16)
            r0 = (j * ah_n + h) * rows
            yr[r0:r0 + rows, :] = y[:rows]
            yi[r0:r0 + rows, :] = y[rows:]
    dg = d // groups
    f = jnp.concatenate(
        [_dot(yr[:, g * dg:(g + 1) * dg], cc_ref[...]) + _dot(yi[:, g * dg:(g + 1) * dg], sc_ref[...])
         for g in range(groups)], axis=1)
    m = _dot(f.astype(BF16), w_ref[...]) + b_ref[...]
    for j in range(kb_n):
        for h in range(ah_n):
            r0 = (j * ah_n + h) * rows
            o_ref[:, j, h] = x_ref[:, j, h] + (gate * m[r0:r0 + rows]).reshape(c_n, SUBLANES, d)


def _fourier_layer(x, mod, w, layer, j, *, rows3=512):
    bt, s, d = x.shape
    a_n, b_n, c_n = _fft_factors(s)
    assert a_n * b_n * c_n == s and a_n % 8 == 0 and c_n % 8 == 0 and d % FNET_GROUPS == 0
    ah_n, ch_n = a_n // 8, c_n // 8
    dg = d // FNET_GROUPS
    m1, (cm2, sm2), (cm3, sm3), tw1, tw2, cc, sc = _fft_tables(s, dg)
    mod_spec = pl.BlockSpec((None, 1, N_ADA * d), lambda *i: (i[0], 0, 0))

    x1 = x.reshape(bt, a_n, b_n * ch_n, 8, d)
    t1_shape = jax.ShapeDtypeStruct((bt, ah_n, ch_n, 8, b_n, 8, d), F32)
    t1_spec = pl.BlockSpec((None, ah_n, None, 8, None, 8, d),
                           lambda i, q: (i, 0, q % ch_n, 0, q // ch_n, 0, 0))
    t1r, t1i = pl.pallas_call(
        _fft1_kernel,
        out_shape=(t1_shape, t1_shape),
        grid=(bt, b_n * ch_n),
        in_specs=[
            pl.BlockSpec((None, a_n, None, 8, d), lambda i, q: (i, 0, q, 0, 0)),
            mod_spec,
            _layer_slice((1, d), layer),
            _resident(m1.shape),
        ],
        out_specs=(t1_spec, t1_spec),
        compiler_params=_params(2),
        name="fourier_seq1",
    )(x1, mod, w["norm_mix"], m1)

    t1r = t1r.reshape(bt, ah_n, c_n, b_n, 8, d)
    t1i = t1i.reshape(bt, ah_n, c_n, b_n, 8, d)
    t1_in = pl.BlockSpec((None, None, None, b_n, 8, d), lambda i, h, c: (i, h, c, 0, 0, 0))
    t2_shape = jax.ShapeDtypeStruct((bt, b_n, ah_n, c_n, 8, d), F32)
    t2_spec = pl.BlockSpec((None, b_n, None, None, 8, d), lambda i, h, c: (i, 0, h, c, 0, 0))
    tw1_spec = pl.BlockSpec((None, 1, b_n * 8), lambda i, h, c: (h, 0, 0))
    t2r, t2i = pl.pallas_call(
        _fft2_kernel,
        out_shape=(t2_shape, t2_shape),
        grid=(bt, ah_n, c_n),
        in_specs=[t1_in, t1_in, _resident(cm2.shape), _resident(sm2.shape), tw1_spec, tw1_spec],
        out_specs=(t2_spec, t2_spec),
        scratch_shapes=[pltpu.VMEM((2 * b_n * 8, 2 * b_n * 8), BF16)],
        compiler_params=_params(3),
        name="fourier_seq2",
    )(t1r, t1i, cm2, sm2, *tw1)

    kb_blk = max(1, min(b_n, rows3 // (ah_n * c_n * 8)))
    while b_n % kb_blk:
        kb_blk -= 1
    rows = kb_blk * ah_n * c_n * 8
    t2_in = pl.BlockSpec((None, kb_blk, ah_n, c_n, 8, d), lambda i, k: (i, k, 0, 0, 0, 0))
    xo_spec = pl.BlockSpec((None, c_n, kb_blk, ah_n, 8, d), lambda i, k: (i, 0, k, 0, 0, 0))
    tw2_spec = pl.BlockSpec((kb_blk, ah_n, 1, c_n * 8), lambda i, k: (k, 0, 0, 0))
    out = pl.pallas_call(
        functools.partial(_fft3_kernel, groups=FNET_GROUPS),
        out_shape=jax.ShapeDtypeStruct((bt, c_n, b_n, ah_n, 8, d), F32),
        grid=(bt, b_n // kb_blk),
        in_specs=[
            t2_in, t2_in, xo_spec, mod_spec,
            _resident(cm3.shape), _resident(sm3.shape), tw2_spec, tw2_spec,
            _resident((dg, dg)), _resident((dg, dg)),
            _layer_slice((d, d), j),
            _layer_slice((1, d), j),
        ],
        out_specs=xo_spec,
        scratch_shapes=[pltpu.VMEM((rows, d), BF16), pltpu.VMEM((rows, d), BF16)],
        compiler_params=_params(2),
        name="fourier_seq3",
    )(t2r, t2i, x.reshape(bt, c_n, b_n, ah_n, 8, d), mod, cm3, sm3, *tw2, cc, sc,
      w["b_w_out"], w["b_b_out"])
    return out.reshape(bt, s, d)


def _tile_rows(s, target):
    tm = min(s, target)
    while s % tm:
        tm //= 2
    return tm


def _trunk(x, mods, w, *, mlp_tm=512, conv_tm=512, ff_chunk=1024):
    depth = w["norm_mix"].shape[0]
    s = x.shape[1]
    for i in range(depth):
        kind, j = i % N_MIXERS, i // N_MIXERS
        if kind == 0:
            x = _sconv_layer(x, mods[i], w, i, j, tm=_tile_rows(s, conv_tm))
        elif kind == 1:
            x = _fourier_layer(x, mods[i], w, i, j)
        else:
            x = _conformer_layer(x, mods[i], w, i, j, tm=_tile_rows(s, conv_tm))
        x = _mlp_layer(x, mods[i], w, i, final=(i == depth - 1), tm=_tile_rows(s, mlp_tm),
                       ff_chunk=ff_chunk)
    return x


def kernel(x_prompt, x_sample, c_prompt, c_sample, ada_w, ada_b, norm_mix, norm_mlp, a_w_in, a_conv_w, a_w_out, b_w_out, b_b_out, c_w_pw1, c_b_pw1, c_dw_w, c_dw_b, c_ln_g, c_ln_b, c_w_pw2, c_b_pw2, mlp_w_up, mlp_w_down, final_norm):
    d = x_prompt.shape[-1]
    n_p, n_s = c_prompt.shape[0], c_sample.shape[0]
    pad_rows = -(n_p + n_s) % SUBLANES
    c_all = jnp.concatenate([c_prompt, c_sample, jnp.zeros((pad_rows, d), F32)], axis=0)
    mod = _ada_modulation(c_all, ada_w, ada_b)
    mods_p = mod[:, :n_p, None, :]
    mods_s = mod[:, n_p:n_p + n_s, None, :]
    w = dict(
        norm_mix=_rows(norm_mix), norm_mlp=_rows(norm_mlp), final_norm=final_norm.reshape(1, d),
        a_w_in=a_w_in.astype(BF16), a_conv_w=a_conv_w, a_w_out=a_w_out.astype(BF16),
        b_w_out=b_w_out.astype(BF16), b_b_out=_rows(b_b_out),
        c_w_pw1=c_w_pw1.astype(BF16), c_b_pw1=_rows(c_b_pw1), c_dw_w=c_dw_w, c_dw_b=_rows(c_dw_b),
        c_ln_g=_rows(c_ln_g), c_ln_b=_rows(c_ln_b), c_w_pw2=c_w_pw2.astype(BF16), c_b_pw2=_rows(c_b_pw2),
        mlp_w_up=mlp_w_up.astype(BF16), mlp_w_down=mlp_w_down.astype(BF16),
    )
    return _trunk(x_prompt, mods_p, w), _trunk(x_sample, mods_s, w)
```

```python
import functools
import math

import jax
import jax.numpy as jnp
from jax import lax
from jax.experimental import pallas as pl
from jax.experimental.pallas import tpu as pltpu

EPS = 1e-6
N_ADA = 6
N_MIXERS = 3
FNET_GROUPS = 4

SUBLANES = 8
LANES = 128
BF16_ROWS = 16
VMEM_LIMIT_BYTES = 56 * 1024 * 1024

F32 = jnp.float32
BF16 = jnp.bfloat16
U32 = jnp.uint32


def _params(n_axes):
    return pltpu.CompilerParams(
        dimension_semantics=("arbitrary",) * n_axes, vmem_limit_bytes=VMEM_LIMIT_BYTES)


def _resident(shape):
    zeros = (0,) * len(shape)
    return pl.BlockSpec(shape, lambda *_: zeros, pipeline_mode=pl.Buffered(1))


def _layer_slice(shape, layer):
    index = (layer,) + (0,) * len(shape)
    return pl.BlockSpec((None,) + shape, lambda *_: index, pipeline_mode=pl.Buffered(1))


def _rows(p):
    return p.reshape(p.shape[0], 1, p.shape[1])


def _rms_mod(x, gain, scale, shift):
    ms = jnp.mean(x * x, axis=-1, keepdims=True)
    return (x * lax.rsqrt(ms + EPS)) * gain * (1.0 + scale) + shift


def _dot(a, b):
    return jnp.dot(a, b, preferred_element_type=F32)


def _ada_kernel(c_ref, w_ref, b_ref, o_ref):
    c = c_ref[...]
    c_act = (c * jax.nn.sigmoid(c)).astype(BF16)
    o_ref[...] = _dot(c_act, w_ref[...].astype(BF16)) + b_ref[...]


def _ada_modulation(c_all, ada_w, ada_b, col_block=1536):
    depth, d, n = ada_w.shape
    rows = c_all.shape[0]
    assert n % col_block == 0
    return pl.pallas_call(
        _ada_kernel,
        out_shape=jax.ShapeDtypeStruct((depth, rows, n), F32),
        grid=(depth, n // col_block),
        in_specs=[
            pl.BlockSpec((rows, d), lambda l, j: (0, 0)),
            pl.BlockSpec((None, d, col_block), lambda l, j: (l, 0, j)),
            pl.BlockSpec((None, 1, col_block), lambda l, j: (l, 0, j)),
        ],
        out_specs=pl.BlockSpec((None, rows, col_block), lambda l, j: (l, 0, j)),
        compiler_params=_params(2),
        name="ada_modulation",
    )(c_all, ada_w, _rows(ada_b))


def _mlp_kernel(x_ref, mod_ref, g_ref, wu_ref, wd_ref, fn_ref, o_ref, *, ff_chunk, final):
    x = x_ref[...]
    d = x.shape[-1]
    shift, scale, gate = (mod_ref[:, k * d:(k + 1) * d] for k in (3, 4, 5))
    h = _rms_mod(x, g_ref[...], scale, shift).astype(BF16)
    acc = jnp.zeros_like(x)
    for j in range(wu_ref.shape[1] // ff_chunk):
        u = jnp.maximum(_dot(h, wu_ref[:, j * ff_chunk:(j + 1) * ff_chunk]), 0.0)
        acc = acc + _dot((u * u).astype(BF16), wd_ref[j * ff_chunk:(j + 1) * ff_chunk, :])
    y = x + gate * acc
    if final:
        y = y * lax.rsqrt(jnp.mean(y * y, axis=-1, keepdims=True) + EPS) * fn_ref[...]
    o_ref[...] = y


def _mlp_layer(x, mod, w, layer, *, final, tm, ff_chunk):
    bt, s, d = x.shape
    f = w["mlp_w_up"].shape[-1]
    ff_chunk = min(ff_chunk, f)
    assert s % tm == 0 and f % ff_chunk == 0
    tiles = s // tm
    xf = x.reshape(bt * s, d)
    out = pl.pallas_call(
        functools.partial(_mlp_kernel, ff_chunk=ff_chunk, final=final),
        out_shape=jax.ShapeDtypeStruct(xf.shape, F32),
        grid=(bt * tiles,),
        in_specs=[
            pl.BlockSpec((tm, d), lambda i: (i, 0)),
            pl.BlockSpec((None, 1, N_ADA * d), lambda i: (i // tiles, 0, 0)),
            _layer_slice((1, d), layer),
            _layer_slice((d, f), layer),
            _layer_slice((f, d), layer),
            _resident((1, d)),
        ],
        out_specs=pl.BlockSpec((tm, d), lambda i: (i, 0)),
        compiler_params=_params(1),
        name="mlp_final" if final else "mlp",
    )(xf, mod, w["norm_mlp"], w["mlp_w_up"], w["mlp_w_down"], w["final_norm"])
    return out.reshape(bt, s, d)


HALO = BF16_ROWS


def _halo_specs(tm, d, n_rows):
    per = tm // HALO
    last = n_rows // HALO - 1
    return [
        pl.BlockSpec((HALO, d), lambda i: (jnp.maximum(i * per - 1, 0), 0)),
        pl.BlockSpec((tm, d), lambda i: (i, 0)),
        pl.BlockSpec((HALO, d), lambda i: (jnp.minimum((i + 1) * per, last), 0)),
    ]


def _fill_hext(hext, xp_ref, x_ref, xn_ref, gain, scale, shift, tm):
    hext[0:HALO, :] = _rms_mod(xp_ref[...], gain, scale, shift).astype(BF16)
    hext[HALO:HALO + tm, :] = _rms_mod(x_ref[...], gain, scale, shift).astype(BF16)
    hext[HALO + tm:, :] = _rms_mod(xn_ref[...], gain, scale, shift).astype(BF16)


def _zero_outside_sequence(u, tm, tiles):
    t = pl.program_id(0) % tiles
    row = lax.broadcasted_iota(jnp.int32, u.shape, 0)
    outside = ((t == 0) & (row < HALO)) | ((t == tiles - 1) & (row >= HALO + tm))
    return jnp.where(outside, 0.0, u)


def _sconv_kernel(xp_ref, x_ref, xn_ref, mod_ref, g_ref, win_ref, cw_ref, wout_ref, o_ref, hext,
                  *, tm, tiles):
    d = x_ref.shape[-1]
    n = tm + 2 * HALO
    shift, scale, gate = (mod_ref[:, k * d:(k + 1) * d] for k in (0, 1, 2))
    _fill_hext(hext, xp_ref, x_ref, xn_ref, g_ref[...], scale, shift, tm)
    cv = _dot(hext[...], win_ref[:, d:3 * d])
    u = _zero_outside_sequence(cv[:, :d] * cv[:, d:], tm, tiles)
    prev = pltpu.roll(u, 1, axis=0)[HALO:HALO + tm]
    nxt = pltpu.roll(u, n - 1, axis=0)[HALO:HALO + tm]
    y = cw_ref[0:1, :] * prev + cw_ref[1:2, :] * u[HALO:HALO + tm] + cw_ref[2:3, :] * nxt
    b = _dot(hext[HALO:HALO + tm, :], win_ref[:, 0:d])
    m = _dot((b * y).astype(BF16), wout_ref[...])
    o_ref[...] = x_ref[...] + gate * m


def _sconv_layer(x, mod, w, layer, j, *, tm):
    bt, s, d = x.shape
    assert s % tm == 0 and tm % HALO == 0 and w["a_conv_w"].shape[1] == 3
    tiles = s // tm
    xf = x.reshape(bt * s, d)
    out = pl.pallas_call(
        functools.partial(_sconv_kernel, tm=tm, tiles=tiles),
        out_shape=jax.ShapeDtypeStruct(xf.shape, F32),
        grid=(bt * tiles,),
        in_specs=_halo_specs(tm, d, bt * s) + [
            pl.BlockSpec((None, 1, N_ADA * d), lambda i: (i // tiles, 0, 0)),
            _layer_slice((1, d), layer),
            _layer_slice((d, 3 * d), j),
            _layer_slice((3, d), j),
            _layer_slice((d, d), j),
        ],
        out_specs=pl.BlockSpec((tm, d), lambda i: (i, 0)),
        scratch_shapes=[pltpu.VMEM((tm + 2 * HALO, d), BF16)],
        compiler_params=_params(1),
        name="short_conv",
    )(xf, xf, xf, mod, w["norm_mix"], w["a_w_in"], w["a_conv_w"], w["a_w_out"])
    return out.reshape(bt, s, d)


CONV_ROWS = 4 * BF16_ROWS


def _conformer_kernel(xp_ref, x_ref, xn_ref, mod_ref, g_ref, w1_ref, b1_ref, dw_ref, dwb_ref,
                      lng_ref, lnb_ref, w2_ref, b2_ref, o_ref, hext, even, odd, wtap, conv,
                      *, tm, tiles, width):
    d = x_ref.shape[-1]
    n = tm + 2 * HALO
    pad = width // 2
    lane_tiles = d // LANES

    @pl.when(pl.program_id(0) == 0)
    def _():
        for k in range(width):
            for j in range(lane_tiles):
                tap = dw_ref[k:k + 1, j * LANES:(j + 1) * LANES]
                wtap[k, j] = jnp.broadcast_to(tap, (BF16_ROWS, LANES)).astype(BF16)

    shift, scale, gate = (mod_ref[:, k * d:(k + 1) * d] for k in (0, 1, 2))
    _fill_hext(hext, xp_ref, x_ref, xn_ref, g_ref[...], scale, shift, tm)
    ag = _dot(hext[...], w1_ref[...]) + b1_ref[...]
    u = _zero_outside_sequence(ag[:, :d] * jax.nn.sigmoid(ag[:, d:]), tm, tiles)
    pe = pltpu.bitcast(u.astype(BF16), U32)
    po = pltpu.bitcast(pltpu.roll(u, n - 1, axis=0).astype(BF16), U32)
    for j in range(lane_tiles):
        even[j] = pe[:, j * LANES:(j + 1) * LANES]
        odd[j] = po[:, j * LANES:(j + 1) * LANES]

    half = BF16_ROWS // 2

    def chunk(ci, carry):
        word0 = pl.multiple_of(ci * (CONV_ROWS // 2), CONV_ROWS // 2)
        row0 = pl.multiple_of(ci * CONV_ROWS, CONV_ROWS)
        for j in range(lane_tiles):
            acc = [jnp.zeros((BF16_ROWS, LANES), F32) for _ in range(CONV_ROWS // BF16_ROWS)]
            for k in range(width):
                off = HALO - pad + k
                src = odd if off % 2 else even
                tap = wtap[k, j].astype(F32)
                for a in range(len(acc)):
                    words = src[j, pl.ds(word0 + off // 2 + a * half, half), :]
                    acc[a] = acc[a] + tap * pltpu.bitcast(words, BF16).astype(F32)
            bias = dwb_ref[:, j * LANES:(j + 1) * LANES]
            for a in range(len(acc)):
                conv[pl.ds(row0 + a * BF16_ROWS, BF16_ROWS), j * LANES:(j + 1) * LANES] = acc[a] + bias
        return carry

    lax.fori_loop(0, tm // CONV_ROWS, chunk, 0)
    v = conv[...]
    mu = jnp.mean(v, axis=-1, keepdims=True)
    vc = v - mu
    ln = vc * lax.rsqrt(jnp.mean(vc * vc, axis=-1, keepdims=True) + EPS) * lng_ref[...] + lnb_ref[...]
    act = (ln * jax.nn.sigmoid(ln)).astype(BF16)
    m = _dot(act, w2_ref[...]) + b2_ref[...]
    o_ref[...] = x_ref[...] + gate * m


def _conformer_layer(x, mod, w, layer, j, *, tm):
    bt, s, d = x.shape
    width = w["c_dw_w"].shape[1]
    assert s % tm == 0 and tm % CONV_ROWS == 0 and width // 2 < HALO and d % LANES == 0
    tiles = s // tm
    n = tm + 2 * HALO
    xf = x.reshape(bt * s, d)
    out = pl.pallas_call(
        functools.partial(_conformer_kernel, tm=tm, tiles=tiles, width=width),
        out_shape=jax.ShapeDtypeStruct(xf.shape, F32),
        grid=(bt * tiles,),
        in_specs=_halo_specs(tm, d, bt * s) + [
            pl.BlockSpec((None, 1, N_ADA * d), lambda i: (i // tiles, 0, 0)),
            _layer_slice((1, d), layer),
            _layer_slice((d, 2 * d), j),
            _layer_slice((1, 2 * d), j),
            _layer_slice((width, d), j),
            _layer_slice((1, d), j),
            _layer_slice((1, d), j),
            _layer_slice((1, d), j),
            _layer_slice((d, d), j),
            _layer_slice((1, d), j),
        ],
        out_specs=pl.BlockSpec((tm, d), lambda i: (i, 0)),
        scratch_shapes=[
            pltpu.VMEM((n, d), BF16),
            pltpu.VMEM((d // LANES, n // 2, LANES), U32),
            pltpu.VMEM((d // LANES, n // 2, LANES), U32),
            pltpu.VMEM((width, d // LANES, BF16_ROWS, LANES), BF16),
            pltpu.VMEM((tm, d), F32),
        ],
        compiler_params=_params(1),
        name="conformer",
    )(xf, xf, xf, mod, w["norm_mix"], w["c_w_pw1"], w["c_b_pw1"], w["c_dw_w"], w["c_dw_b"],
      w["c_ln_g"], w["c_ln_b"], w["c_w_pw2"], w["c_b_pw2"])
    return out.reshape(bt, s, d)


PAIR = 2 * SUBLANES


def _fft_factors(s):
    known = {4096: (16, 16, 16), 16384: (32, 32, 16)}
    if s in known:
        return known[s]
    best = None
    for a in (8, 16, 32, 64):
        for c in (8, 16, 32, 64):
            if s % (a * c) == 0:
                b = s // (a * c)
                cost = max(a, 32) + 2 * max(b, 16) + 2 * max(c, 16)
                if best is None or cost < best[0]:
                    best = (cost, (a, b, c))
    assert best is not None, f"sequence length {s} needs two factors that are multiples of 8"
    return best[1]


def _largest_divisor(n, limit):
    k = max(1, min(n, limit))
    while n % k:
        k -= 1
    return k


def _cos_sin(phase, period):
    ang = phase.astype(F32) * F32(2.0 * math.pi / period)
    return jnp.cos(ang), jnp.sin(ang)


def _stage_tables(n, part_major_rows):
    size = PAIR * n
    r = lax.broadcasted_iota(jnp.int32, (size, size), 0)
    q = lax.broadcasted_iota(jnp.int32, (size, size), 1)
    m, pi = q // PAIR, (q // SUBLANES) % 2
    if part_major_rows:
        po, k = r // (SUBLANES * n), (r // SUBLANES) % n
    else:
        po, k = (r // SUBLANES) % 2, r // PAIR
    co, si = _cos_sin((k * m) % n, n)
    sign = (pi - po).astype(F32)
    same = (r % SUBLANES) == (q % SUBLANES)
    p = jnp.where(same, jnp.where(po == pi, co, sign * si), 0.0)
    qq = jnp.where(same, jnp.where(po == pi, -si, sign * co), 0.0)
    return p, qq


def _fft_tables(s, dg):
    a_n, b_n, c_n = _fft_factors(s)
    i32 = jnp.int32
    ar = functools.partial(jnp.arange, dtype=i32)

    r = lax.broadcasted_iota(i32, (PAIR * a_n, SUBLANES * a_n), 0)
    q = lax.broadcasted_iota(i32, (PAIR * a_n, SUBLANES * a_n), 1)
    ka = (r // (SUBLANES * PAIR)) * SUBLANES + r % SUBLANES
    co, si = _cos_sin((ka * (q // SUBLANES)) % a_n, a_n)
    m1 = jnp.where(((r // PAIR) % SUBLANES) == (q % SUBLANES),
                   jnp.where((r // SUBLANES) % 2 == 0, co, -si), 0.0).astype(BF16)

    pq2 = _stage_tables(b_n, False)
    pq3 = _stage_tables(c_n, True)

    ka = ar(a_n // 8)[:, None, None, None] * 8 + ar(8)[None, None, None, :]
    both = jnp.zeros((1, 1, 2, 1), i32)
    tw1 = _cos_sin((ar(b_n)[None, :, None, None] * ka + both) % (a_n * b_n), a_n * b_n)
    tw1 = [t.reshape(a_n // 8, 1, PAIR * b_n) for t in tw1]
    kk = ka[None] + a_n * ar(b_n)[:, None, None, None, None]
    tw2 = _cos_sin((ar(c_n)[None, None, :, None, None] * kk + both[None]) % s, s)
    tw2 = [t.reshape(b_n, a_n // 8, 1, PAIR * c_n) for t in tw2]

    co, si = _cos_sin((ar(dg)[:, None] * ar(dg)[None, :]) % dg, dg)
    norm = F32(1.0 / math.sqrt(s * dg))
    return m1, pq2, pq3, tw1, tw2, (co * norm).astype(BF16), (si * norm).astype(BF16)


def _fft1_kernel(x_ref, mod_ref, g_ref, m_ref, o_ref):
    a_n, bg_n, c_n, d = x_ref.shape
    shift, scale = (mod_ref[:, k * d:(k + 1) * d] for k in (0, 1))
    for bg in range(bg_n):
        for ch in range(c_n // SUBLANES):
            rows = slice(ch * SUBLANES, (ch + 1) * SUBLANES)
            xs = x_ref[:, bg, rows, :].reshape(a_n * SUBLANES, d)
            h = _rms_mod(xs, g_ref[...], scale, shift).astype(BF16)
            y = _dot(m_ref[...], h).astype(BF16)
            o_ref[:, rows, bg] = y.reshape(a_n // SUBLANES, SUBLANES, PAIR, d)


def _fft2_kernel(t_ref, p_ref, q_ref, tc_ref, ts_ref, o_ref, mat):
    cg_n, b_n, _, d = t_ref.shape

    @pl.when(pl.program_id(2) == 0)
    def _():
        mat[...] = (p_ref[...] * tc_ref[...] + q_ref[...] * ts_ref[...]).astype(BF16)

    for cg in range(cg_n):
        y = _dot(mat[...], t_ref[cg].reshape(b_n * PAIR, d)).astype(BF16)
        o_ref[:, cg] = y.reshape(b_n, PAIR, d)


def _fft3_kernel(t_ref, x_ref, mod_ref, p_ref, q_ref, tc_ref, ts_ref, cc_ref, sc_ref, w_ref, b_ref,
                 o_ref, yr, yi, *, groups):
    kb_n, ah_n, c_n = t_ref.shape[0], t_ref.shape[1], t_ref.shape[2]
    d = t_ref.shape[-1]
    rows = c_n * SUBLANES
    gate = mod_ref[:, 2 * d:3 * d]
    for j in range(kb_n):
        for h in range(ah_n):
            mat = (p_ref[...] * tc_ref[j, h] + q_ref[...] * ts_ref[j, h]).astype(BF16)
            y = _dot(mat, t_ref[j, h].reshape(c_n * PAIR, d)).astype(BF16)
            r0 = (j * ah_n + h) * rows
            yr[r0:r0 + rows, :] = y[:rows]
            yi[r0:r0 + rows, :] = y[rows:]
    dg = d // groups
    f = jnp.concatenate(
        [_dot(yr[:, g * dg:(g + 1) * dg], cc_ref[...]) + _dot(yi[:, g * dg:(g + 1) * dg], sc_ref[...])
         for g in range(groups)], axis=1)
    m = _dot(f.astype(BF16), w_ref[...]) + b_ref[...]
    for j in range(kb_n):
        for h in range(ah_n):
            r0 = (j * ah_n + h) * rows
            o_ref[:, j, h] = x_ref[:, j, h] + (gate * m[r0:r0 + rows]).reshape(c_n, SUBLANES, d)


def _fourier_layer(x, mod, w, layer, j, *, step_rows=512):
    bt, s, d = x.shape
    a_n, b_n, c_n = _fft_factors(s)
    assert a_n * b_n * c_n == s and a_n % 8 == 0 and c_n % 8 == 0 and d % FNET_GROUPS == 0
    ah_n = a_n // 8
    dg = d // FNET_GROUPS
    m1, pq2, pq3, tw1, tw2, cc, sc = _fft_tables(s, dg)
    mod_spec = pl.BlockSpec((None, 1, N_ADA * d), lambda *i: (i[0], 0, 0))

    bg = _largest_divisor(b_n, step_rows // (a_n * c_n))
    t1 = pl.pallas_call(
        _fft1_kernel,
        out_shape=jax.ShapeDtypeStruct((bt, ah_n, c_n, b_n, PAIR, d), BF16),
        grid=(bt, b_n // bg),
        in_specs=[
            pl.BlockSpec((None, a_n, bg, c_n, d), lambda i, q: (i, 0, q, 0, 0)),
            mod_spec,
            _layer_slice((1, d), layer),
            _resident(m1.shape),
        ],
        out_specs=pl.BlockSpec((None, ah_n, c_n, bg, PAIR, d), lambda i, q: (i, 0, 0, q, 0, 0)),
        compiler_params=_params(2),
        name="fourier_seq1",
    )(x.reshape(bt, a_n, b_n, c_n, d), mod, w["norm_mix"], m1)

    cg = _largest_divisor(c_n, step_rows // (b_n * SUBLANES))
    tw1_spec = pl.BlockSpec((None, 1, PAIR * b_n), lambda i, h, c: (h, 0, 0))
    t2 = pl.pallas_call(
        _fft2_kernel,
        out_shape=jax.ShapeDtypeStruct((bt, b_n, ah_n, c_n, PAIR, d), BF16),
        grid=(bt, ah_n, c_n // cg),
        in_specs=[
            pl.BlockSpec((None, None, cg, b_n, PAIR, d), lambda i, h, c: (i, h, c, 0, 0, 0)),
            _resident(pq2[0].shape), _resident(pq2[1].shape), tw1_spec, tw1_spec,
        ],
        out_specs=pl.BlockSpec((None, b_n, None, cg, PAIR, d), lambda i, h, c: (i, 0, h, c, 0, 0)),
        scratch_shapes=[pltpu.VMEM((PAIR * b_n, PAIR * b_n), BF16)],
        compiler_params=_params(3),
        name="fourier_seq2",
    )(t1, *pq2, *tw1)

    kb_blk = _largest_divisor(b_n, step_rows // (ah_n * c_n * SUBLANES))
    rows = kb_blk * ah_n * c_n * SUBLANES
    xo_spec = pl.BlockSpec((None, c_n, kb_blk, ah_n, SUBLANES, d), lambda i, k: (i, 0, k, 0, 0, 0))
    tw2_spec = pl.BlockSpec((kb_blk, ah_n, 1, PAIR * c_n), lambda i, k: (k, 0, 0, 0))
    out = pl.pallas_call(
        functools.partial(_fft3_kernel, groups=FNET_GROUPS),
        out_shape=jax.ShapeDtypeStruct((bt, c_n, b_n, ah_n, SUBLANES, d), F32),
        grid=(bt, b_n // kb_blk),
        in_specs=[
            pl.BlockSpec((None, kb_blk, ah_n, c_n, PAIR, d), lambda i, k: (i, k, 0, 0, 0, 0)),
            xo_spec, mod_spec,
            _resident(pq3[0].shape), _resident(pq3[1].shape), tw2_spec, tw2_spec,
            _resident((dg, dg)), _resident((dg, dg)),
            _layer_slice((d, d), j),
            _layer_slice((1, d), j),
        ],
        out_specs=xo_spec,
        scratch_shapes=[pltpu.VMEM((rows, d), BF16), pltpu.VMEM((rows, d), BF16)],
        compiler_params=_params(2),
        name="fourier_seq3",
    )(t2, x.reshape(bt, c_n, b_n, ah_n, SUBLANES, d), mod, *pq3, *tw2, cc, sc, w["b_w_out"], w["b_b_out"])
    return out.reshape(bt, s, d)


def _tile_rows(s, target):
    tm = min(s, target)
    while s % tm:
        tm //= 2
    return tm


def _trunk(x, mods, w, *, mlp_tm=1024, conv_tm=512, ff_chunk=1024):
    depth = w["norm_mix"].shape[0]
    s = x.shape[1]
    for i in range(depth):
        kind, j = i % N_MIXERS, i // N_MIXERS
        if kind == 0:
            x = _sconv_layer(x, mods[i], w, i, j, tm=_tile_rows(s, conv_tm))
        elif kind == 1:
            x = _fourier_layer(x, mods[i], w, i, j)
        else:
            x = _conformer_layer(x, mods[i], w, i, j, tm=_tile_rows(s, conv_tm))
        x = _mlp_layer(x, mods[i], w, i, final=(i == depth - 1), tm=_tile_rows(s, mlp_tm),
                       ff_chunk=ff_chunk)
    return x


def kernel(x_prompt, x_sample, c_prompt, c_sample, ada_w, ada_b, norm_mix, norm_mlp, a_w_in, a_conv_w, a_w_out, b_w_out, b_b_out, c_w_pw1, c_b_pw1, c_dw_w, c_dw_b, c_ln_g, c_ln_b, c_w_pw2, c_b_pw2, mlp_w_up, mlp_w_down, final_norm):
    d = x_prompt.shape[-1]
    n_p, n_s = c_prompt.shape[0], c_sample.shape[0]
    pad_rows = -(n_p + n_s) % SUBLANES
    c_all = jnp.concatenate([c_prompt, c_sample, jnp.zeros((pad_rows, d), F32)], axis=0)
    mod = _ada_modulation(c_all, ada_w, ada_b)
    mods_p = mod[:, :n_p, None, :]
    mods_s = mod[:, n_p:n_p + n_s, None, :]
    w = dict(
        norm_mix=_rows(norm_mix), norm_mlp=_rows(norm_mlp), final_norm=final_norm.reshape(1, d),
        a_w_in=a_w_in.astype(BF16), a_conv_w=a_conv_w, a_w_out=a_w_out.astype(BF16),
        b_w_out=b_w_out.astype(BF16), b_b_out=_rows(b_b_out),
        c_w_pw1=c_w_pw1.astype(BF16), c_b_pw1=_rows(c_b_pw1), c_dw_w=c_dw_w, c_dw_b=_rows(c_dw_b),
        c_ln_g=_rows(c_ln_g), c_ln_b=_rows(c_ln_b), c_w_pw2=c_w_pw2.astype(BF16), c_b_pw2=_rows(c_b_pw2),
        mlp_w_up=mlp_w_up.astype(BF16), mlp_w_down=mlp_w_down.astype(BF16),
    )
    return _trunk(x_prompt, mods_p, w), _trunk(x_sample, mods_s, w)
```

```python
import functools
import math

import jax
import jax.numpy as jnp
from jax import lax
from jax.experimental import pallas as pl
from jax.experimental.pallas import tpu as pltpu

EPS = 1e-6
N_ADA = 6
N_MIXERS = 3
FNET_GROUPS = 4

SUBLANES = 8
LANES = 128
BF16_ROWS = 16
VMEM_LIMIT_BYTES = 56 * 1024 * 1024

F32 = jnp.float32
BF16 = jnp.bfloat16
U32 = jnp.uint32


def _params(n_axes):
    return pltpu.CompilerParams(
        dimension_semantics=("arbitrary",) * n_axes, vmem_limit_bytes=VMEM_LIMIT_BYTES)


def _resident(shape):
    zeros = (0,) * len(shape)
    return pl.BlockSpec(shape, lambda *_: zeros, pipeline_mode=pl.Buffered(1))


def _layer_slice(shape, layer):
    index = (layer,) + (0,) * len(shape)
    return pl.BlockSpec((None,) + shape, lambda *_: index, pipeline_mode=pl.Buffered(1))


def _rows(p):
    return p.reshape(p.shape[0], 1, p.shape[1])


def _rms_mod(x, gain, scale, shift):
    ms = jnp.mean(x * x, axis=-1, keepdims=True)
    return (x * lax.rsqrt(ms + EPS)) * gain * (1.0 + scale) + shift


def _dot(a, b):
    return jnp.dot(a, b, preferred_element_type=F32)


def _ada_kernel(c_ref, w_ref, b_ref, o_ref):
    c = c_ref[...]
    c_act = (c * jax.nn.sigmoid(c)).astype(BF16)
    o_ref[...] = _dot(c_act, w_ref[...].astype(BF16)) + b_ref[...]


def _ada_modulation(c_all, ada_w, ada_b, col_block=1536):
    depth, d, n = ada_w.shape
    rows = c_all.shape[0]
    assert n % col_block == 0
    return pl.pallas_call(
        _ada_kernel,
        out_shape=jax.ShapeDtypeStruct((depth, rows, n), F32),
        grid=(depth, n // col_block),
        in_specs=[
            pl.BlockSpec((rows, d), lambda l, j: (0, 0)),
            pl.BlockSpec((None, d, col_block), lambda l, j: (l, 0, j)),
            pl.BlockSpec((None, 1, col_block), lambda l, j: (l, 0, j)),
        ],
        out_specs=pl.BlockSpec((None, rows, col_block), lambda l, j: (l, 0, j)),
        compiler_params=_params(2),
        name="ada_modulation",
    )(c_all, ada_w, _rows(ada_b))


def _mlp_kernel(x_ref, mod_ref, g_ref, wu_ref, wd_ref, fn_ref, o_ref, *, ff_chunk, final):
    x = x_ref[...]
    d = x.shape[-1]
    shift, scale, gate = (mod_ref[:, k * d:(k + 1) * d] for k in (3, 4, 5))
    h = _rms_mod(x, g_ref[...], scale, shift).astype(BF16)
    acc = jnp.zeros_like(x)
    for j in range(wu_ref.shape[1] // ff_chunk):
        u = jnp.maximum(_dot(h, wu_ref[:, j * ff_chunk:(j + 1) * ff_chunk]), 0.0)
        acc = acc + _dot((u * u).astype(BF16), wd_ref[j * ff_chunk:(j + 1) * ff_chunk, :])
    y = x + gate * acc
    if final:
        y = y * lax.rsqrt(jnp.mean(y * y, axis=-1, keepdims=True) + EPS) * fn_ref[...]
    o_ref[...] = y


def _mlp_layer(x, mod, w, layer, *, final, tm, ff_chunk):
    bt, s, d = x.shape
    f = w["mlp_w_up"].shape[-1]
    ff_chunk = min(ff_chunk, f)
    assert s % tm == 0 and f % ff_chunk == 0
    tiles = s // tm
    xf = x.reshape(bt * s, d)
    out = pl.pallas_call(
        functools.partial(_mlp_kernel, ff_chunk=ff_chunk, final=final),
        out_shape=jax.ShapeDtypeStruct(xf.shape, F32),
        grid=(bt * tiles,),
        in_specs=[
            pl.BlockSpec((tm, d), lambda i: (i, 0)),
            pl.BlockSpec((None, 1, N_ADA * d), lambda i: (i // tiles, 0, 0)),
            _layer_slice((1, d), layer),
            _layer_slice((d, f), layer),
            _layer_slice((f, d), layer),
            _resident((1, d)),
        ],
        out_specs=pl.BlockSpec((tm, d), lambda i: (i, 0)),
        compiler_params=_params(1),
        name="mlp_final" if final else "mlp",
    )(xf, mod, w["norm_mlp"], w["mlp_w_up"], w["mlp_w_down"], w["final_norm"])
    return out.reshape(bt, s, d)


HALO = BF16_ROWS


def _halo_specs(tm, d, n_rows):
    per = tm // HALO
    last = n_rows // HALO - 1
    return [
        pl.BlockSpec((HALO, d), lambda i: (jnp.maximum(i * per - 1, 0), 0)),
        pl.BlockSpec((tm, d), lambda i: (i, 0)),
        pl.BlockSpec((HALO, d), lambda i: (jnp.minimum((i + 1) * per, last), 0)),
    ]


def _fill_hext(hext, xp_ref, x_ref, xn_ref, gain, scale, shift, tm):
    hext[0:HALO, :] = _rms_mod(xp_ref[...], gain, scale, shift).astype(BF16)
    hext[HALO:HALO + tm, :] = _rms_mod(x_ref[...], gain, scale, shift).astype(BF16)
    hext[HALO + tm:, :] = _rms_mod(xn_ref[...], gain, scale, shift).astype(BF16)


def _zero_outside_sequence(u, tm, tiles):
    t = pl.program_id(0) % tiles
    row = lax.broadcasted_iota(jnp.int32, u.shape, 0)
    outside = ((t == 0) & (row < HALO)) | ((t == tiles - 1) & (row >= HALO + tm))
    return jnp.where(outside, 0.0, u)


def _sconv_kernel(xp_ref, x_ref, xn_ref, mod_ref, g_ref, win_ref, cw_ref, wout_ref, o_ref, hext,
                  *, tm, tiles):
    d = x_ref.shape[-1]
    n = tm + 2 * HALO
    shift, scale, gate = (mod_ref[:, k * d:(k + 1) * d] for k in (0, 1, 2))
    _fill_hext(hext, xp_ref, x_ref, xn_ref, g_ref[...], scale, shift, tm)
    cv = _dot(hext[...], win_ref[:, d:3 * d])
    u = _zero_outside_sequence(cv[:, :d] * cv[:, d:], tm, tiles)
    prev = pltpu.roll(u, 1, axis=0)[HALO:HALO + tm]
    nxt = pltpu.roll(u, n - 1, axis=0)[HALO:HALO + tm]
    y = cw_ref[0:1, :] * prev + cw_ref[1:2, :] * u[HALO:HALO + tm] + cw_ref[2:3, :] * nxt
    b = _dot(hext[HALO:HALO + tm, :], win_ref[:, 0:d])
    m = _dot((b * y).astype(BF16), wout_ref[...])
    o_ref[...] = x_ref[...] + gate * m


def _sconv_layer(x, mod, w, layer, j, *, tm):
    bt, s, d = x.shape
    assert s % tm == 0 and tm % HALO == 0 and w["a_conv_w"].shape[1] == 3
    tiles = s // tm
    xf = x.reshape(bt * s, d)
    out = pl.pallas_call(
        functools.partial(_sconv_kernel, tm=tm, tiles=tiles),
        out_shape=jax.ShapeDtypeStruct(xf.shape, F32),
        grid=(bt * tiles,),
        in_specs=_halo_specs(tm, d, bt * s) + [
            pl.BlockSpec((None, 1, N_ADA * d), lambda i: (i // tiles, 0, 0)),
            _layer_slice((1, d), layer),
            _layer_slice((d, 3 * d), j),
            _layer_slice((3, d), j),
            _layer_slice((d, d), j),
        ],
        out_specs=pl.BlockSpec((tm, d), lambda i: (i, 0)),
        scratch_shapes=[pltpu.VMEM((tm + 2 * HALO, d), BF16)],
        compiler_params=_params(1),
        name="short_conv",
    )(xf, xf, xf, mod, w["norm_mix"], w["a_w_in"], w["a_conv_w"], w["a_w_out"])
    return out.reshape(bt, s, d)


CONV_ROWS = 4 * BF16_ROWS


def _conformer_kernel(xp_ref, x_ref, xn_ref, mod_ref, g_ref, w1_ref, b1_ref, dw_ref, dwb_ref,
                      lng_ref, lnb_ref, w2_ref, b2_ref, o_ref, hext, even, odd, wtap, conv,
                      *, tm, tiles, width):
    d = x_ref.shape[-1]
    n = tm + 2 * HALO
    pad = width // 2
    lane_tiles = d // LANES

    @pl.when(pl.program_id(0) == 0)
    def _():
        for k in range(width):
            for j in range(lane_tiles):
                tap = dw_ref[k:k + 1, j * LANES:(j + 1) * LANES]
                wtap[k, j] = jnp.broadcast_to(tap, (BF16_ROWS, LANES)).astype(BF16)

    shift, scale, gate = (mod_ref[:, k * d:(k + 1) * d] for k in (0, 1, 2))
    _fill_hext(hext, xp_ref, x_ref, xn_ref, g_ref[...], scale, shift, tm)
    ag = _dot(hext[...], w1_ref[...]) + b1_ref[...]
    u = _zero_outside_sequence(ag[:, :d] * jax.nn.sigmoid(ag[:, d:]), tm, tiles)
    pe = pltpu.bitcast(u.astype(BF16), U32)
    po = pltpu.bitcast(pltpu.roll(u, n - 1, axis=0).astype(BF16), U32)
    for j in range(lane_tiles):
        even[j] = pe[:, j * LANES:(j + 1) * LANES]
        odd[j] = po[:, j * LANES:(j + 1) * LANES]

    half = BF16_ROWS // 2

    def chunk(ci, carry):
        word0 = pl.multiple_of(ci * (CONV_ROWS // 2), CONV_ROWS // 2)
        row0 = pl.multiple_of(ci * CONV_ROWS, CONV_ROWS)
        for j in range(lane_tiles):
            acc = [jnp.zeros((BF16_ROWS, LANES), F32) for _ in range(CONV_ROWS // BF16_ROWS)]
            for k in range(width):
                off = HALO - pad + k
                src = odd if off % 2 else even
                tap = wtap[k, j].astype(F32)
                for a in range(len(acc)):
                    words = src[j, pl.ds(word0 + off // 2 + a * half, half), :]
                    acc[a] = acc[a] + tap * pltpu.bitcast(words, BF16).astype(F32)
            bias = dwb_ref[:, j * LANES:(j + 1) * LANES]
            for a in range(len(acc)):
                conv[pl.ds(row0 + a * BF16_ROWS, BF16_ROWS), j * LANES:(j + 1) * LANES] = acc[a] + bias
        return carry

    lax.fori_loop(0, tm // CONV_ROWS, chunk, 0)
    v = conv[...]
    mu = jnp.mean(v, axis=-1, keepdims=True)
    vc = v - mu
    ln = vc * lax.rsqrt(jnp.mean(vc * vc, axis=-1, keepdims=True) + EPS) * lng_ref[...] + lnb_ref[...]
    act = (ln * jax.nn.sigmoid(ln)).astype(BF16)
    m = _dot(act, w2_ref[...]) + b2_ref[...]
    o_ref[...] = x_ref[...] + gate * m


def _conformer_layer(x, mod, w, layer, j, *, tm):
    bt, s, d = x.shape
    width = w["c_dw_w"].shape[1]
    assert s % tm == 0 and tm % CONV_ROWS == 0 and width // 2 < HALO and d % LANES == 0
    tiles = s // tm
    n = tm + 2 * HALO
    xf = x.reshape(bt * s, d)
    out = pl.pallas_call(
        functools.partial(_conformer_kernel, tm=tm, tiles=tiles, width=width),
        out_shape=jax.ShapeDtypeStruct(xf.shape, F32),
        grid=(bt * tiles,),
        in_specs=_halo_specs(tm, d, bt * s) + [
            pl.BlockSpec((None, 1, N_ADA * d), lambda i: (i // tiles, 0, 0)),
            _layer_slice((1, d), layer),
            _layer_slice((d, 2 * d), j),
            _layer_slice((1, 2 * d), j),
            _layer_slice((width, d), j),
            _layer_slice((1, d), j),
            _layer_slice((1, d), j),
            _layer_slice((1, d), j),
            _layer_slice((d, d), j),
            _layer_slice((1, d), j),
        ],
        out_specs=pl.BlockSpec((tm, d), lambda i: (i, 0)),
        scratch_shapes=[
            pltpu.VMEM((n, d), BF16),
            pltpu.VMEM((d // LANES, n // 2, LANES), U32),
            pltpu.VMEM((d // LANES, n // 2, LANES), U32),
            pltpu.VMEM((width, d // LANES, BF16_ROWS, LANES), BF16),
            pltpu.VMEM((tm, d), F32),
        ],
        compiler_params=_params(1),
        name="conformer",
    )(xf, xf, xf, mod, w["norm_mix"], w["c_w_pw1"], w["c_b_pw1"], w["c_dw_w"], w["c_dw_b"],
      w["c_ln_g"], w["c_ln_b"], w["c_w_pw2"], w["c_b_pw2"])
    return out.reshape(bt, s, d)


PAIR = 2 * SUBLANES


def _fft_factors(s):
    known = {4096: (16, 16, 16), 16384: (32, 32, 16)}
    if s in known:
        return known[s]
    best = None
    for a in (8, 16, 32, 64):
        for c in (8, 16, 32, 64):
            if s % (a * c) == 0:
                b = s // (a * c)
                cost = max(a, 32) + 2 * max(b, 16) + 2 * max(c, 16)
                if best is None or cost < best[0]:
                    best = (cost, (a, b, c))
    assert best is not None, f"sequence length {s} needs two factors that are multiples of 8"
    return best[1]


def _largest_divisor(n, limit):
    k = max(1, min(n, limit))
    while n % k:
        k -= 1
    return k


def _cos_sin(phase, period):
    ang = phase.astype(F32) * F32(2.0 * math.pi / period)
    return jnp.cos(ang), jnp.sin(ang)


def _stage_tables(n, part_major_rows):
    size = PAIR * n
    r = lax.broadcasted_iota(jnp.int32, (size, size), 0)
    q = lax.broadcasted_iota(jnp.int32, (size, size), 1)
    m, pi = q // PAIR, (q // SUBLANES) % 2
    if part_major_rows:
        po, k = r // (SUBLANES * n), (r // SUBLANES) % n
    else:
        po, k = (r // SUBLANES) % 2, r // PAIR
    co, si = _cos_sin((k * m) % n, n)
    sign = (pi - po).astype(F32)
    same = (r % SUBLANES) == (q % SUBLANES)
    p = jnp.where(same, jnp.where(po == pi, co, sign * si), 0.0)
    qq = jnp.where(same, jnp.where(po == pi, -si, sign * co), 0.0)
    return p, qq


def _fft_tables(s, dg):
    a_n, b_n, c_n = _fft_factors(s)
    i32 = jnp.int32
    ar = functools.partial(jnp.arange, dtype=i32)

    r = lax.broadcasted_iota(i32, (PAIR * a_n, SUBLANES * a_n), 0)
    q = lax.broadcasted_iota(i32, (PAIR * a_n, SUBLANES * a_n), 1)
    ka = (r // (SUBLANES * PAIR)) * SUBLANES + r % SUBLANES
    co, si = _cos_sin((ka * (q // SUBLANES)) % a_n, a_n)
    m1 = jnp.where(((r // PAIR) % SUBLANES) == (q % SUBLANES),
                   jnp.where((r // SUBLANES) % 2 == 0, co, -si), 0.0).astype(BF16)

    pq2 = _stage_tables(b_n, False)
    pq3 = _stage_tables(c_n, True)

    ka = ar(a_n // 8)[:, None, None, None] * 8 + ar(8)[None, None, None, :]
    both = jnp.zeros((1, 1, 2, 1), i32)
    tw1 = _cos_sin((ar(b_n)[None, :, None, None] * ka + both) % (a_n * b_n), a_n * b_n)
    tw1 = [t.reshape(a_n // 8, 1, PAIR * b_n) for t in tw1]
    kk = ka[None] + a_n * ar(b_n)[:, None, None, None, None]
    tw2 = _cos_sin((ar(c_n)[None, None, :, None, None] * kk + both[None]) % s, s)
    tw2 = [t.reshape(b_n, a_n // 8, 1, PAIR * c_n) for t in tw2]

    co, si = _cos_sin((ar(dg)[:, None] * ar(dg)[None, :]) % dg, dg)
    norm = F32(1.0 / math.sqrt(s * dg))
    return m1, pq2, pq3, tw1, tw2, (co * norm).astype(BF16), (si * norm).astype(BF16)


def _fft1_kernel(x_ref, mod_ref, g_ref, m_ref, o_ref):
    a_n, bg_n, c_n, d = x_ref.shape
    shift, scale = (mod_ref[:, k * d:(k + 1) * d] for k in (0, 1))
    for bg in range(bg_n):
        for ch in range(c_n // SUBLANES):
            rows = slice(ch * SUBLANES, (ch + 1) * SUBLANES)
            xs = x_ref[:, bg, rows, :].reshape(a_n * SUBLANES, d)
            h = _rms_mod(xs, g_ref[...], scale, shift).astype(BF16)
            y = _dot(m_ref[...], h).astype(BF16)
            o_ref[:, rows, bg] = y.reshape(a_n // SUBLANES, SUBLANES, PAIR, d)


def _fft2_kernel(t_ref, p_ref, q_ref, tc_ref, ts_ref, o_ref, mat):
    cg_n, b_n, _, d = t_ref.shape

    @pl.when(pl.program_id(2) == 0)
    def _():
        mat[...] = (p_ref[...] * tc_ref[...] + q_ref[...] * ts_ref[...]).astype(BF16)

    for cg in range(cg_n):
        y = _dot(mat[...], t_ref[cg].reshape(b_n * PAIR, d)).astype(BF16)
        o_ref[:, cg] = y.reshape(b_n, PAIR, d)


def _fft3_kernel(t_ref, x_ref, mod_ref, p_ref, q_ref, tc_ref, ts_ref, cc_ref, sc_ref, w_ref, b_ref,
                 o_ref, yr, yi, *, groups):
    kb_n, ah_n, c_n = t_ref.shape[0], t_ref.shape[1], t_ref.shape[2]
    d = t_ref.shape[-1]
    rows = c_n * SUBLANES
    gate = mod_ref[:, 2 * d:3 * d]
    for j in range(kb_n):
        for h in range(ah_n):
            mat = (p_ref[...] * tc_ref[j, h] + q_ref[...] * ts_ref[j, h]).astype(BF16)
            y = _dot(mat, t_ref[j, h].reshape(c_n * PAIR, d)).astype(BF16)
            r0 = (j * ah_n + h) * rows
            yr[r0:r0 + rows, :] = y[:rows]
            yi[r0:r0 + rows, :] = y[rows:]
    dg = d // groups
    f = jnp.concatenate(
        [_dot(yr[:, g * dg:(g + 1) * dg], cc_ref[...]) + _dot(yi[:, g * dg:(g + 1) * dg], sc_ref[...])
         for g in range(groups)], axis=1)
    m = _dot(f.astype(BF16), w_ref[...]) + b_ref[...]
    for j in range(kb_n):
        for h in range(ah_n):
            r0 = (j * ah_n + h) * rows
            o_ref[:, j, h] = x_ref[:, j, h] + (gate * m[r0:r0 + rows]).reshape(c_n, SUBLANES, d)


def _fourier_layer(x, mod, w, layer, j, *, step_rows=1024):
    bt, s, d = x.shape
    a_n, b_n, c_n = _fft_factors(s)
    assert a_n * b_n * c_n == s and a_n % 8 == 0 and c_n % 8 == 0 and d % FNET_GROUPS == 0
    ah_n = a_n // 8
    dg = d // FNET_GROUPS
    m1, pq2, pq3, tw1, tw2, cc, sc = _fft_tables(s, dg)
    mod_spec = pl.BlockSpec((None, 1, N_ADA * d), lambda *i: (i[0], 0, 0))

    bg = _largest_divisor(b_n, step_rows // (a_n * c_n))
    t1 = pl.pallas_call(
        _fft1_kernel,
        out_shape=jax.ShapeDtypeStruct((bt, ah_n, c_n, b_n, PAIR, d), BF16),
        grid=(bt, b_n // bg),
        in_specs=[
            pl.BlockSpec((None, a_n, bg, c_n, d), lambda i, q: (i, 0, q, 0, 0)),
            mod_spec,
            _layer_slice((1, d), layer),
            _resident(m1.shape),
        ],
        out_specs=pl.BlockSpec((None, ah_n, c_n, bg, PAIR, d), lambda i, q: (i, 0, 0, q, 0, 0)),
        compiler_params=_params(2),
        name="fourier_seq1",
    )(x.reshape(bt, a_n, b_n, c_n, d), mod, w["norm_mix"], m1)

    cg = _largest_divisor(c_n, step_rows // (b_n * SUBLANES))
    tw1_spec = pl.BlockSpec((None, 1, PAIR * b_n), lambda i, h, c: (h, 0, 0))
    t2 = pl.pallas_call(
        _fft2_kernel,
        out_shape=jax.ShapeDtypeStruct((bt, b_n, ah_n, c_n, PAIR, d), BF16),
        grid=(bt, ah_n, c_n // cg),
        in_specs=[
            pl.BlockSpec((None, None, cg, b_n, PAIR, d), lambda i, h, c: (i, h, c, 0, 0, 0)),
            _resident(pq2[0].shape), _resident(pq2[1].shape), tw1_spec, tw1_spec,
        ],
        out_specs=pl.BlockSpec((None, b_n, None, cg, PAIR, d), lambda i, h, c: (i, 0, h, c, 0, 0)),
        scratch_shapes=[pltpu.VMEM((PAIR * b_n, PAIR * b_n), BF16)],
        compiler_params=_params(3),
        name="fourier_seq2",
    )(t1, *pq2, *tw1)

    kb_blk = _largest_divisor(b_n, step_rows // (ah_n * c_n * SUBLANES))
    rows = kb_blk * ah_n * c_n * SUBLANES
    xo_spec = pl.BlockSpec((None, c_n, kb_blk, ah_n, SUBLANES, d), lambda i, k: (i, 0, k, 0, 0, 0))
    tw2_spec = pl.BlockSpec((kb_blk, ah_n, 1, PAIR * c_n), lambda i, k: (k, 0, 0, 0))
    out = pl.pallas_call(
        functools.partial(_fft3_kernel, groups=FNET_GROUPS),
        out_shape=jax.ShapeDtypeStruct((bt, c_n, b_n, ah_n, SUBLANES, d), F32),
        grid=(bt, b_n // kb_blk),
        in_specs=[
            pl.BlockSpec((None, kb_blk, ah_n, c_n, PAIR, d), lambda i, k: (i, k, 0, 0, 0, 0)),
            xo_spec, mod_spec,
            _resident(pq3[0].shape), _resident(pq3[1].shape), tw2_spec, tw2_spec,
            _resident((dg, dg)), _resident((dg, dg)),
            _layer_slice((d, d), j),
            _layer_slice((1, d), j),
        ],
        out_specs=xo_spec,
        scratch_shapes=[pltpu.VMEM((rows, d), BF16), pltpu.VMEM((rows, d), BF16)],
        compiler_params=_params(2),
        name="fourier_seq3",
    )(t2, x.reshape(bt, c_n, b_n, ah_n, SUBLANES, d), mod, *pq3, *tw2, cc, sc, w["b_w_out"], w["b_b_out"])
    return out.reshape(bt, s, d)


def _tile_rows(s, target):
    tm = min(s, target)
    while s % tm:
        tm //= 2
    return tm


def _trunk(x, mods, w, *, mlp_tm=1024, conv_tm=1024, ff_chunk=1024):
    depth = w["norm_mix"].shape[0]
    s = x.shape[1]
    for i in range(depth):
        kind, j = i % N_MIXERS, i // N_MIXERS
        if kind == 0:
            x = _sconv_layer(x, mods[i], w, i, j, tm=_tile_rows(s, conv_tm))
        elif kind == 1:
            x = _fourier_layer(x, mods[i], w, i, j)
        else:
            x = _conformer_layer(x, mods[i], w, i, j, tm=_tile_rows(s, conv_tm))
        x = _mlp_layer(x, mods[i], w, i, final=(i == depth - 1), tm=_tile_rows(s, mlp_tm),
                       ff_chunk=ff_chunk)
    return x


def kernel(x_prompt, x_sample, c_prompt, c_sample, ada_w, ada_b, norm_mix, norm_mlp, a_w_in, a_conv_w, a_w_out, b_w_out, b_b_out, c_w_pw1, c_b_pw1, c_dw_w, c_dw_b, c_ln_g, c_ln_b, c_w_pw2, c_b_pw2, mlp_w_up, mlp_w_down, final_norm):
    d = x_prompt.shape[-1]
    n_p, n_s = c_prompt.shape[0], c_sample.shape[0]
    pad_rows = -(n_p + n_s) % SUBLANES
    c_all = jnp.concatenate([c_prompt, c_sample, jnp.zeros((pad_rows, d), F32)], axis=0)
    mod = _ada_modulation(c_all, ada_w, ada_b)
    mods_p = mod[:, :n_p, None, :]
    mods_s = mod[:, n_p:n_p + n_s, None, :]
    w = dict(
        norm_mix=_rows(norm_mix), norm_mlp=_rows(norm_mlp), final_norm=final_norm.reshape(1, d),
        a_w_in=a_w_in.astype(BF16), a_conv_w=a_conv_w, a_w_out=a_w_out.astype(BF16),
        b_w_out=b_w_out.astype(BF16), b_b_out=_rows(b_b_out),
        c_w_pw1=c_w_pw1.astype(BF16), c_b_pw1=_rows(c_b_pw1), c_dw_w=c_dw_w, c_dw_b=_rows(c_dw_b),
        c_ln_g=_rows(c_ln_g), c_ln_b=_rows(c_ln_b), c_w_pw2=c_w_pw2.astype(BF16), c_b_pw2=_rows(c_b_pw2),
        mlp_w_up=mlp_w_up.astype(BF16), mlp_w_down=mlp_w_down.astype(BF16),
    )
    return _trunk(x_prompt, mods_p, w), _trunk(x_sample, mods_s, w)
```

```python
import functools
import math

import jax
import jax.numpy as jnp
from jax import lax
from jax.experimental import pallas as pl
from jax.experimental.pallas import tpu as pltpu

EPS = 1e-6
N_ADA = 6
N_MIXERS = 3
FNET_GROUPS = 4

SUBLANES = 8
LANES = 128
BF16_ROWS = 16
VMEM_LIMIT_BYTES = 56 * 1024 * 1024

F32 = jnp.float32
BF16 = jnp.bfloat16
U32 = jnp.uint32


def _params(n_axes):
    return pltpu.CompilerParams(
        dimension_semantics=("arbitrary",) * n_axes, vmem_limit_bytes=VMEM_LIMIT_BYTES)


def _resident(shape):
    zeros = (0,) * len(shape)
    return pl.BlockSpec(shape, lambda *_: zeros, pipeline_mode=pl.Buffered(1))


def _layer_slice(shape, layer):
    index = (layer,) + (0,) * len(shape)
    return pl.BlockSpec((None,) + shape, lambda *_: index, pipeline_mode=pl.Buffered(1))


def _rows(p):
    return p.reshape(p.shape[0], 1, p.shape[1])


def _rms_mod(x, gain, scale, shift):
    ms = jnp.mean(x * x, axis=-1, keepdims=True)
    return (x * lax.rsqrt(ms + EPS)) * gain * (1.0 + scale) + shift


def _dot(a, b):
    return jnp.dot(a, b, preferred_element_type=F32)


def _ada_kernel(c_ref, w_ref, b_ref, o_ref):
    c = c_ref[...]
    c_act = (c * jax.nn.sigmoid(c)).astype(BF16)
    o_ref[...] = _dot(c_act, w_ref[...].astype(BF16)) + b_ref[...]


def _ada_modulation(c_all, ada_w, ada_b, col_block=1536):
    depth, d, n = ada_w.shape
    rows = c_all.shape[0]
    assert n % col_block == 0
    return pl.pallas_call(
        _ada_kernel,
        out_shape=jax.ShapeDtypeStruct((depth, rows, n), F32),
        grid=(depth, n // col_block),
        in_specs=[
            pl.BlockSpec((rows, d), lambda l, j: (0, 0)),
            pl.BlockSpec((None, d, col_block), lambda l, j: (l, 0, j)),
            pl.BlockSpec((None, 1, col_block), lambda l, j: (l, 0, j)),
        ],
        out_specs=pl.BlockSpec((None, rows, col_block), lambda l, j: (l, 0, j)),
        compiler_params=_params(2),
        name="ada_modulation",
    )(c_all, ada_w, _rows(ada_b))


def _mlp_kernel(x_ref, mod_ref, g_ref, wu_ref, wd_ref, fn_ref, o_ref, *, ff_chunk, final):
    x = x_ref[...]
    d = x.shape[-1]
    shift, scale, gate = (mod_ref[:, k * d:(k + 1) * d] for k in (3, 4, 5))
    h = _rms_mod(x, g_ref[...], scale, shift).astype(BF16)
    acc = jnp.zeros_like(x)
    for j in range(wu_ref.shape[1] // ff_chunk):
        u = jnp.maximum(_dot(h, wu_ref[:, j * ff_chunk:(j + 1) * ff_chunk]), 0.0)
        acc = acc + _dot((u * u).astype(BF16), wd_ref[j * ff_chunk:(j + 1) * ff_chunk, :])
    y = x + gate * acc
    if final:
        y = y * lax.rsqrt(jnp.mean(y * y, axis=-1, keepdims=True) + EPS) * fn_ref[...]
    o_ref[...] = y


def _mlp_layer(x, mod, w, layer, *, final, tm, ff_chunk):
    bt, s, d = x.shape
    f = w["mlp_w_up"].shape[-1]
    ff_chunk = min(ff_chunk, f)
    assert s % tm == 0 and f % ff_chunk == 0
    tiles = s // tm
    xf = x.reshape(bt * s, d)
    out = pl.pallas_call(
        functools.partial(_mlp_kernel, ff_chunk=ff_chunk, final=final),
        out_shape=jax.ShapeDtypeStruct(xf.shape, F32),
        grid=(bt * tiles,),
        in_specs=[
            pl.BlockSpec((tm, d), lambda i: (i, 0)),
            pl.BlockSpec((None, 1, N_ADA * d), lambda i: (i // tiles, 0, 0)),
            _layer_slice((1, d), layer),
            _layer_slice((d, f), layer),
            _layer_slice((f, d), layer),
            _resident((1, d)),
        ],
        out_specs=pl.BlockSpec((tm, d), lambda i: (i, 0)),
        compiler_params=_params(1),
        name="mlp_final" if final else "mlp",
    )(xf, mod, w["norm_mlp"], w["mlp_w_up"], w["mlp_w_down"], w["final_norm"])
    return out.reshape(bt, s, d)


HALO = BF16_ROWS


def _halo_specs(tm, d, n_rows):
    per = tm // HALO
    last = n_rows // HALO - 1
    return [
        pl.BlockSpec((HALO, d), lambda i: (jnp.maximum(i * per - 1, 0), 0)),
        pl.BlockSpec((tm, d), lambda i: (i, 0)),
        pl.BlockSpec((HALO, d), lambda i: (jnp.minimum((i + 1) * per, last), 0)),
    ]


def _fill_hext(hext, xp_ref, x_ref, xn_ref, gain, scale, shift, tm):
    hext[0:HALO, :] = _rms_mod(xp_ref[...], gain, scale, shift).astype(BF16)
    hext[HALO:HALO + tm, :] = _rms_mod(x_ref[...], gain, scale, shift).astype(BF16)
    hext[HALO + tm:, :] = _rms_mod(xn_ref[...], gain, scale, shift).astype(BF16)


def _zero_outside_sequence(u, tm, tiles):
    t = pl.program_id(0) % tiles
    before = jnp.where(t == 0, 0.0, u[:HALO])
    after = jnp.where(t == tiles - 1, 0.0, u[HALO + tm:])
    return jnp.concatenate([before, u[HALO:HALO + tm], after], axis=0)


def _sconv_kernel(xp_ref, x_ref, xn_ref, mod_ref, g_ref, win_ref, cw_ref, wout_ref, o_ref, hext,
                  *, tm, tiles):
    d = x_ref.shape[-1]
    n = tm + 2 * HALO
    shift, scale, gate = (mod_ref[:, k * d:(k + 1) * d] for k in (0, 1, 2))
    _fill_hext(hext, xp_ref, x_ref, xn_ref, g_ref[...], scale, shift, tm)
    cv = _dot(hext[...], win_ref[:, d:3 * d])
    u = _zero_outside_sequence(cv[:, :d] * cv[:, d:], tm, tiles)
    prev = pltpu.roll(u, 1, axis=0)[HALO:HALO + tm]
    nxt = pltpu.roll(u, n - 1, axis=0)[HALO:HALO + tm]
    y = cw_ref[0:1, :] * prev + cw_ref[1:2, :] * u[HALO:HALO + tm] + cw_ref[2:3, :] * nxt
    b = _dot(hext[HALO:HALO + tm, :], win_ref[:, 0:d])
    m = _dot((b * y).astype(BF16), wout_ref[...])
    o_ref[...] = x_ref[...] + gate * m


def _sconv_layer(x, mod, w, layer, j, *, tm):
    bt, s, d = x.shape
    assert s % tm == 0 and tm % HALO == 0 and w["a_conv_w"].shape[1] == 3
    tiles = s // tm
    xf = x.reshape(bt * s, d)
    out = pl.pallas_call(
        functools.partial(_sconv_kernel, tm=tm, tiles=tiles),
        out_shape=jax.ShapeDtypeStruct(xf.shape, F32),
        grid=(bt * tiles,),
        in_specs=_halo_specs(tm, d, bt * s) + [
            pl.BlockSpec((None, 1, N_ADA * d), lambda i: (i // tiles, 0, 0)),
            _layer_slice((1, d), layer),
            _layer_slice((d, 3 * d), j),
            _layer_slice((3, d), j),
            _layer_slice((d, d), j),
        ],
        out_specs=pl.BlockSpec((tm, d), lambda i: (i, 0)),
        scratch_shapes=[pltpu.VMEM((tm + 2 * HALO, d), BF16)],
        compiler_params=_params(1),
        name="short_conv",
    )(xf, xf, xf, mod, w["norm_mix"], w["a_w_in"], w["a_conv_w"], w["a_w_out"])
    return out.reshape(bt, s, d)


CONV_ROWS = 4 * BF16_ROWS


def _conformer_kernel(xp_ref, x_ref, xn_ref, mod_ref, g_ref, w1_ref, b1_ref, dw_ref, dwb_ref,
                      lng_ref, lnb_ref, w2_ref, b2_ref, o_ref, hext, even, odd, wtap, conv,
                      *, tm, tiles, width):
    d = x_ref.shape[-1]
    n = tm + 2 * HALO
    pad = width // 2
    lane_tiles = d // LANES

    @pl.when(pl.program_id(0) == 0)
    def _():
        for k in range(width):
            for j in range(lane_tiles):
                tap = dw_ref[k:k + 1, j * LANES:(j + 1) * LANES]
                wtap[k, j] = jnp.broadcast_to(tap, (BF16_ROWS, LANES)).astype(BF16)

    shift, scale, gate = (mod_ref[:, k * d:(k + 1) * d] for k in (0, 1, 2))
    _fill_hext(hext, xp_ref, x_ref, xn_ref, g_ref[...], scale, shift, tm)
    ag = _dot(hext[...], w1_ref[...]) + b1_ref[...]
    u = _zero_outside_sequence(ag[:, :d] * jax.nn.sigmoid(ag[:, d:]), tm, tiles)
    pe = pltpu.bitcast(u.astype(BF16), U32)
    po = pltpu.bitcast(pltpu.roll(u, n - 1, axis=0).astype(BF16), U32)
    for j in range(lane_tiles):
        even[j] = pe[:, j * LANES:(j + 1) * LANES]
        odd[j] = po[:, j * LANES:(j + 1) * LANES]

    half = BF16_ROWS // 2

    def chunk(ci, carry):
        word0 = pl.multiple_of(ci * (CONV_ROWS // 2), CONV_ROWS // 2)
        row0 = pl.multiple_of(ci * CONV_ROWS, CONV_ROWS)
        for j in range(lane_tiles):
            acc = [jnp.zeros((BF16_ROWS, LANES), F32) for _ in range(CONV_ROWS // BF16_ROWS)]
            for k in range(width):
                off = HALO - pad + k
                src = odd if off % 2 else even
                tap = wtap[k, j].astype(F32)
                for a in range(len(acc)):
                    words = src[j, pl.ds(word0 + off // 2 + a * half, half), :]
                    acc[a] = acc[a] + tap * pltpu.bitcast(words, BF16).astype(F32)
            bias = dwb_ref[:, j * LANES:(j + 1) * LANES]
            for a in range(len(acc)):
                conv[pl.ds(row0 + a * BF16_ROWS, BF16_ROWS), j * LANES:(j + 1) * LANES] = acc[a] + bias
        return carry

    lax.fori_loop(0, tm // CONV_ROWS, chunk, 0)
    v = conv[...]
    mu = jnp.mean(v, axis=-1, keepdims=True)
    vc = v - mu
    ln = vc * lax.rsqrt(jnp.mean(vc * vc, axis=-1, keepdims=True) + EPS) * lng_ref[...] + lnb_ref[...]
    act = (ln * jax.nn.sigmoid(ln)).astype(BF16)
    m = _dot(act, w2_ref[...]) + b2_ref[...]
    o_ref[...] = x_ref[...] + gate * m


def _conformer_layer(x, mod, w, layer, j, *, tm):
    bt, s, d = x.shape
    width = w["c_dw_w"].shape[1]
    assert s % tm == 0 and tm % CONV_ROWS == 0 and width // 2 < HALO and d % LANES == 0
    tiles = s // tm
    n = tm + 2 * HALO
    xf = x.reshape(bt * s, d)
    out = pl.pallas_call(
        functools.partial(_conformer_kernel, tm=tm, tiles=tiles, width=width),
        out_shape=jax.ShapeDtypeStruct(xf.shape, F32),
        grid=(bt * tiles,),
        in_specs=_halo_specs(tm, d, bt * s) + [
            pl.BlockSpec((None, 1, N_ADA * d), lambda i: (i // tiles, 0, 0)),
            _layer_slice((1, d), layer),
            _layer_slice((d, 2 * d), j),
            _layer_slice((1, 2 * d), j),
            _layer_slice((width, d), j),
            _layer_slice((1, d), j),
            _layer_slice((1, d), j),
            _layer_slice((1, d), j),
            _layer_slice((d, d), j),
            _layer_slice((1, d), j),
        ],
        out_specs=pl.BlockSpec((tm, d), lambda i: (i, 0)),
        scratch_shapes=[
            pltpu.VMEM((n, d), BF16),
            pltpu.VMEM((d // LANES, n // 2, LANES), U32),
            pltpu.VMEM((d // LANES, n // 2, LANES), U32),
            pltpu.VMEM((width, d // LANES, BF16_ROWS, LANES), BF16),
            pltpu.VMEM((tm, d), F32),
        ],
        compiler_params=_params(1),
        name="conformer",
    )(xf, xf, xf, mod, w["norm_mix"], w["c_w_pw1"], w["c_b_pw1"], w["c_dw_w"], w["c_dw_b"],
      w["c_ln_g"], w["c_ln_b"], w["c_w_pw2"], w["c_b_pw2"])
    return out.reshape(bt, s, d)


PAIR = 2 * SUBLANES


def _fft_factors(s):
    known = {4096: (16, 16, 16), 16384: (32, 32, 16)}
    if s in known:
        return known[s]
    best = None
    for a in (8, 16, 32, 64):
        for c in (8, 16, 32, 64):
            if s % (a * c) == 0:
                b = s // (a * c)
                cost = max(a, 32) + 2 * max(b, 16) + 2 * max(c, 16)
                if best is None or cost < best[0]:
                    best = (cost, (a, b, c))
    assert best is not None, f"sequence length {s} needs two factors that are multiples of 8"
    return best[1]


def _largest_divisor(n, limit):
    k = max(1, min(n, limit))
    while n % k:
        k -= 1
    return k


def _cos_sin(phase, period):
    ang = phase.astype(F32) * F32(2.0 * math.pi / period)
    return jnp.cos(ang), jnp.sin(ang)


def _stage_tables(n, part_major_rows):
    size = PAIR * n
    r = lax.broadcasted_iota(jnp.int32, (size, size), 0)
    q = lax.broadcasted_iota(jnp.int32, (size, size), 1)
    m, pi = q // PAIR, (q // SUBLANES) % 2
    if part_major_rows:
        po, k = r // (SUBLANES * n), (r // SUBLANES) % n
    else:
        po, k = (r // SUBLANES) % 2, r // PAIR
    co, si = _cos_sin((k * m) % n, n)
    sign = (pi - po).astype(F32)
    same = (r % SUBLANES) == (q % SUBLANES)
    p = jnp.where(same, jnp.where(po == pi, co, sign * si), 0.0)
    qq = jnp.where(same, jnp.where(po == pi, -si, sign * co), 0.0)
    return p, qq


def _fft_tables(s, dg):
    a_n, b_n, c_n = _fft_factors(s)
    i32 = jnp.int32
    ar = functools.partial(jnp.arange, dtype=i32)

    r = lax.broadcasted_iota(i32, (PAIR * a_n, SUBLANES * a_n), 0)
    q = lax.broadcasted_iota(i32, (PAIR * a_n, SUBLANES * a_n), 1)
    ka = (r // (SUBLANES * PAIR)) * SUBLANES + r % SUBLANES
    co, si = _cos_sin((ka * (q // SUBLANES)) % a_n, a_n)
    m1 = jnp.where(((r // PAIR) % SUBLANES) == (q % SUBLANES),
                   jnp.where((r // SUBLANES) % 2 == 0, co, -si), 0.0).astype(BF16)

    pq2 = _stage_tables(b_n, False)
    pq3 = _stage_tables(c_n, True)

    ka = ar(a_n // 8)[:, None, None, None] * 8 + ar(8)[None, None, None, :]
    both = jnp.zeros((1, 1, 2, 1), i32)
    tw1 = _cos_sin((ar(b_n)[None, :, None, None] * ka + both) % (a_n * b_n), a_n * b_n)
    tw1 = [t.reshape(a_n // 8, 1, PAIR * b_n) for t in tw1]
    kk = ka[None] + a_n * ar(b_n)[:, None, None, None, None]
    tw2 = _cos_sin((ar(c_n)[None, None, :, None, None] * kk + both[None]) % s, s)
    tw2 = [t.reshape(b_n, a_n // 8, 1, PAIR * c_n) for t in tw2]

    co, si = _cos_sin((ar(dg)[:, None] * ar(dg)[None, :]) % dg, dg)
    norm = F32(1.0 / math.sqrt(s * dg))
    return m1, pq2, pq3, tw1, tw2, (co * norm).astype(BF16), (si * norm).astype(BF16)


def _fft1_kernel(x_ref, mod_ref, g_ref, m_ref, o_ref):
    a_n, bg_n, c_n, d = x_ref.shape
    shift, scale = (mod_ref[:, k * d:(k + 1) * d] for k in (0, 1))
    for bg in range(bg_n):
        for ch in range(c_n // SUBLANES):
            rows = slice(ch * SUBLANES, (ch + 1) * SUBLANES)
            xs = x_ref[:, bg, rows, :].reshape(a_n * SUBLANES, d)
            h = _rms_mod(xs, g_ref[...], scale, shift).astype(BF16)
            y = _dot(m_ref[...], h).astype(BF16)
            o_ref[:, rows, bg] = y.reshape(a_n // SUBLANES, SUBLANES, PAIR, d)


def _fft2_kernel(t_ref, p_ref, q_ref, tc_ref, ts_ref, o_ref, mat):
    cg_n, b_n, _, d = t_ref.shape

    @pl.when(pl.program_id(2) == 0)
    def _():
        mat[...] = (p_ref[...] * tc_ref[...] + q_ref[...] * ts_ref[...]).astype(BF16)

    for cg in range(cg_n):
        y = _dot(mat[...], t_ref[cg].reshape(b_n * PAIR, d)).astype(BF16)
        o_ref[:, cg] = y.reshape(b_n, PAIR, d)


def _fft12_kernel(x_ref, mod_ref, g_ref, m_ref, p_ref, q_ref, tc_ref, ts_ref, o_ref, t1, mat):
    a_n, b_n, _, d = x_ref.shape
    ah_n = a_n // SUBLANES

    @pl.when((pl.program_id(0) == 0) & (pl.program_id(1) == 0))
    def _():
        for h in range(ah_n):
            mat[h] = (p_ref[...] * tc_ref[h] + q_ref[...] * ts_ref[h]).astype(BF16)

    shift, scale = (mod_ref[:, k * d:(k + 1) * d] for k in (0, 1))
    for b in range(b_n):
        hb = _rms_mod(x_ref[:, b].reshape(a_n * SUBLANES, d), g_ref[...], scale, shift).astype(BF16)
        y = _dot(m_ref[...], hb).astype(BF16)
        t1[:, :, b] = y.reshape(ah_n, SUBLANES, PAIR, d)
    for h in range(ah_n):
        for cl in range(SUBLANES):
            y = _dot(mat[h], t1[h, cl].reshape(b_n * PAIR, d)).astype(BF16)
            o_ref[:, h, cl] = y.reshape(b_n, PAIR, d)


def _fft3_kernel(t_ref, x_ref, mod_ref, p_ref, q_ref, tc_ref, ts_ref, cc_ref, sc_ref, w_ref, b_ref,
                 o_ref, yr, yi, *, groups):
    kb_n, ah_n, c_n = t_ref.shape[0], t_ref.shape[1], t_ref.shape[2]
    d = t_ref.shape[-1]
    rows = c_n * SUBLANES
    gate = mod_ref[:, 2 * d:3 * d]
    for j in range(kb_n):
        for h in range(ah_n):
            mat = (p_ref[...] * tc_ref[j, h] + q_ref[...] * ts_ref[j, h]).astype(BF16)
            y = _dot(mat, t_ref[j, h].reshape(c_n * PAIR, d)).astype(BF16)
            r0 = (j * ah_n + h) * rows
            yr[r0:r0 + rows, :] = y[:rows]
            yi[r0:r0 + rows, :] = y[rows:]
    dg = d // groups
    f = jnp.concatenate(
        [_dot(yr[:, g * dg:(g + 1) * dg], cc_ref[...]) + _dot(yi[:, g * dg:(g + 1) * dg], sc_ref[...])
         for g in range(groups)], axis=1)
    m = _dot(f.astype(BF16), w_ref[...]) + b_ref[...]
    for j in range(kb_n):
        for h in range(ah_n):
            r0 = (j * ah_n + h) * rows
            o_ref[:, j, h] = x_ref[:, j, h] + (gate * m[r0:r0 + rows]).reshape(c_n, SUBLANES, d)


def _fourier_layer(x, mod, w, layer, j, *, step_rows=1024):
    bt, s, d = x.shape
    a_n, b_n, c_n = _fft_factors(s)
    assert a_n * b_n * c_n == s and a_n % 8 == 0 and c_n % 8 == 0 and d % FNET_GROUPS == 0
    ah_n = a_n // 8
    dg = d // FNET_GROUPS
    m1, pq2, pq3, tw1, tw2, cc, sc = _fft_tables(s, dg)
    mod_spec = pl.BlockSpec((None, 1, N_ADA * d), lambda *i: (i[0], 0, 0))

    slab_bytes = a_n * b_n * SUBLANES * d * 4
    if 5 * slab_bytes <= VMEM_LIMIT_BYTES - 8 * 1024 * 1024:
        ch_n = c_n // SUBLANES
        t2 = pl.pallas_call(
            _fft12_kernel,
            out_shape=jax.ShapeDtypeStruct((bt, b_n, ah_n, ch_n, SUBLANES, PAIR, d), BF16),
            grid=(bt, ch_n),
            in_specs=[
                pl.BlockSpec((None, a_n, b_n, None, SUBLANES, d), lambda i, c: (i, 0, 0, c, 0, 0)),
                mod_spec,
                _layer_slice((1, d), layer),
                _resident(m1.shape),
                _resident(pq2[0].shape), _resident(pq2[1].shape),
                _resident(tw1[0].shape), _resident(tw1[1].shape),
            ],
            out_specs=pl.BlockSpec((None, b_n, ah_n, None, SUBLANES, PAIR, d),
                                   lambda i, c: (i, 0, 0, c, 0, 0, 0)),
            scratch_shapes=[pltpu.VMEM((ah_n, SUBLANES, b_n, PAIR, d), BF16),
                            pltpu.VMEM((ah_n, PAIR * b_n, PAIR * b_n), BF16)],
            compiler_params=_params(2),
            name="fourier_seq12",
        )(x.reshape(bt, a_n, b_n, ch_n, SUBLANES, d), mod, w["norm_mix"], m1, *pq2, *tw1)
        t2 = t2.reshape(bt, b_n, ah_n, c_n, PAIR, d)
    else:
        t2 = _fourier_stages_1_2(x, mod, w, layer, (a_n, b_n, c_n), m1, pq2, tw1, mod_spec, step_rows)
    return _fourier_stage_3(x, mod, w, j, (a_n, b_n, c_n), t2, pq3, tw2, cc, sc, mod_spec, step_rows)


def _fourier_stages_1_2(x, mod, w, layer, factors, m1, pq2, tw1, mod_spec, step_rows):
    bt, s, d = x.shape
    a_n, b_n, c_n = factors
    ah_n = a_n // 8

    bg = _largest_divisor(b_n, step_rows // (a_n * c_n))
    t1 = pl.pallas_call(
        _fft1_kernel,
        out_shape=jax.ShapeDtypeStruct((bt, ah_n, c_n, b_n, PAIR, d), BF16),
        grid=(bt, b_n // bg),
        in_specs=[
            pl.BlockSpec((None, a_n, bg, c_n, d), lambda i, q: (i, 0, q, 0, 0)),
            mod_spec,
            _layer_slice((1, d), layer),
            _resident(m1.shape),
        ],
        out_specs=pl.BlockSpec((None, ah_n, c_n, bg, PAIR, d), lambda i, q: (i, 0, 0, q, 0, 0)),
        compiler_params=_params(2),
        name="fourier_seq1",
    )(x.reshape(bt, a_n, b_n, c_n, d), mod, w["norm_mix"], m1)

    cg = _largest_divisor(c_n, step_rows // (b_n * SUBLANES))
    tw1_spec = pl.BlockSpec((None, 1, PAIR * b_n), lambda i, h, c: (h, 0, 0))
    return pl.pallas_call(
        _fft2_kernel,
        out_shape=jax.ShapeDtypeStruct((bt, b_n, ah_n, c_n, PAIR, d), BF16),
        grid=(bt, ah_n, c_n // cg),
        in_specs=[
            pl.BlockSpec((None, None, cg, b_n, PAIR, d), lambda i, h, c: (i, h, c, 0, 0, 0)),
            _resident(pq2[0].shape), _resident(pq2[1].shape), tw1_spec, tw1_spec,
        ],
        out_specs=pl.BlockSpec((None, b_n, None, cg, PAIR, d), lambda i, h, c: (i, 0, h, c, 0, 0)),
        scratch_shapes=[pltpu.VMEM((PAIR * b_n, PAIR * b_n), BF16)],
        compiler_params=_params(3),
        name="fourier_seq2",
    )(t1, *pq2, *tw1)


def _fourier_stage_3(x, mod, w, j, factors, t2, pq3, tw2, cc, sc, mod_spec, step_rows):
    bt, s, d = x.shape
    a_n, b_n, c_n = factors
    ah_n = a_n // 8
    dg = d // FNET_GROUPS

    kb_blk = _largest_divisor(b_n, step_rows // (ah_n * c_n * SUBLANES))
    rows = kb_blk * ah_n * c_n * SUBLANES
    xo_spec = pl.BlockSpec((None, c_n, kb_blk, ah_n, SUBLANES, d), lambda i, k: (i, 0, k, 0, 0, 0))
    tw2_spec = pl.BlockSpec((kb_blk, ah_n, 1, PAIR * c_n), lambda i, k: (k, 0, 0, 0))
    out = pl.pallas_call(
        functools.partial(_fft3_kernel, groups=FNET_GROUPS),
        out_shape=jax.ShapeDtypeStruct((bt, c_n, b_n, ah_n, SUBLANES, d), F32),
        grid=(bt, b_n // kb_blk),
        in_specs=[
            pl.BlockSpec((None, kb_blk, ah_n, c_n, PAIR, d), lambda i, k: (i, k, 0, 0, 0, 0)),
            xo_spec, mod_spec,
            _resident(pq3[0].shape), _resident(pq3[1].shape), tw2_spec, tw2_spec,
            _resident((dg, dg)), _resident((dg, dg)),
            _layer_slice((d, d), j),
            _layer_slice((1, d), j),
        ],
        out_specs=xo_spec,
        scratch_shapes=[pltpu.VMEM((rows, d), BF16), pltpu.VMEM((rows, d), BF16)],
        compiler_params=_params(2),
        name="fourier_seq3",
    )(t2, x.reshape(bt, c_n, b_n, ah_n, SUBLANES, d), mod, *pq3, *tw2, cc, sc, w["b_w_out"], w["b_b_out"])
    return out.reshape(bt, s, d)


def _tile_rows(s, target):
    tm = min(s, target)
    while s % tm:
        tm //= 2
    return tm


def _trunk(x, mods, w, *, mlp_tm=1024, conv_tm=1024, ff_chunk=1024):
    depth = w["norm_mix"].shape[0]
    s = x.shape[1]
    for i in range(depth):
        kind, j = i % N_MIXERS, i // N_MIXERS
        if kind == 0:
            x = _sconv_layer(x, mods[i], w, i, j, tm=_tile_rows(s, conv_tm))
        elif kind == 1:
            x = _fourier_layer(x, mods[i], w, i, j)
        else:
            x = _conformer_layer(x, mods[i], w, i, j, tm=_tile_rows(s, conv_tm))
        x = _mlp_layer(x, mods[i], w, i, final=(i == depth - 1), tm=_tile_rows(s, mlp_tm),
                       ff_chunk=ff_chunk)
    return x


def kernel(x_prompt, x_sample, c_prompt, c_sample, ada_w, ada_b, norm_mix, norm_mlp, a_w_in, a_conv_w, a_w_out, b_w_out, b_b_out, c_w_pw1, c_b_pw1, c_dw_w, c_dw_b, c_ln_g, c_ln_b, c_w_pw2, c_b_pw2, mlp_w_up, mlp_w_down, final_norm):
    d = x_prompt.shape[-1]
    n_p, n_s = c_prompt.shape[0], c_sample.shape[0]
    pad_rows = -(n_p + n_s) % SUBLANES
    c_all = jnp.concatenate([c_prompt, c_sample, jnp.zeros((pad_rows, d), F32)], axis=0)
    mod = _ada_modulation(c_all, ada_w, ada_b)
    mods_p = mod[:, :n_p, None, :]
    mods_s = mod[:, n_p:n_p + n_s, None, :]
    w = dict(
        norm_mix=_rows(norm_mix), norm_mlp=_rows(norm_mlp), final_norm=final_norm.reshape(1, d),
        a_w_in=a_w_in.astype(BF16), a_conv_w=a_conv_w, a_w_out=a_w_out.astype(BF16),
        b_w_out=b_w_out.astype(BF16), b_b_out=_rows(b_b_out),
        c_w_pw1=c_w_pw1.astype(BF16), c_b_pw1=_rows(c_b_pw1), c_dw_w=c_dw_w, c_dw_b=_rows(c_dw_b),
        c_ln_g=_rows(c_ln_g), c_ln_b=_rows(c_ln_b), c_w_pw2=c_w_pw2.astype(BF16), c_b_pw2=_rows(c_b_pw2),
        mlp_w_up=mlp_w_up.astype(BF16), mlp_w_down=mlp_w_down.astype(BF16),
    )
    return _trunk(x_prompt, mods_p, w), _trunk(x_sample, mods_s, w)
```

```python
import functools
import math

import jax
import jax.numpy as jnp
from jax import lax
from jax.experimental import pallas as pl
from jax.experimental.pallas import tpu as pltpu

EPS = 1e-6
N_ADA = 6
N_MIXERS = 3
FNET_GROUPS = 4

SUBLANES = 8
LANES = 128
BF16_ROWS = 16
VMEM_LIMIT_BYTES = 56 * 1024 * 1024

F32 = jnp.float32
BF16 = jnp.bfloat16
U32 = jnp.uint32


def _params(n_axes):
    return pltpu.CompilerParams(
        dimension_semantics=("arbitrary",) * n_axes, vmem_limit_bytes=VMEM_LIMIT_BYTES)


def _resident(shape):
    zeros = (0,) * len(shape)
    return pl.BlockSpec(shape, lambda *_: zeros, pipeline_mode=pl.Buffered(1))


def _layer_slice(shape, layer):
    index = (layer,) + (0,) * len(shape)
    return pl.BlockSpec((None,) + shape, lambda *_: index, pipeline_mode=pl.Buffered(1))


def _rows(p):
    return p.reshape(p.shape[0], 1, p.shape[1])


def _rms_mod(x, gain, scale, shift):
    ms = jnp.mean(x * x, axis=-1, keepdims=True)
    return (x * lax.rsqrt(ms + EPS)) * gain * (1.0 + scale) + shift


def _dot(a, b):
    return jnp.dot(a, b, preferred_element_type=F32)


def _ada_kernel(c_ref, w_ref, b_ref, o_ref):
    c = c_ref[...]
    c_act = (c * jax.nn.sigmoid(c)).astype(BF16)
    o_ref[...] = _dot(c_act, w_ref[...].astype(BF16)) + b_ref[...]


def _ada_modulation(c_all, ada_w, ada_b, col_block=1536):
    depth, d, n = ada_w.shape
    rows = c_all.shape[0]
    assert n % col_block == 0
    return pl.pallas_call(
        _ada_kernel,
        out_shape=jax.ShapeDtypeStruct((depth, rows, n), F32),
        grid=(depth, n // col_block),
        in_specs=[
            pl.BlockSpec((rows, d), lambda l, j: (0, 0)),
            pl.BlockSpec((None, d, col_block), lambda l, j: (l, 0, j)),
            pl.BlockSpec((None, 1, col_block), lambda l, j: (l, 0, j)),
        ],
        out_specs=pl.BlockSpec((None, rows, col_block), lambda l, j: (l, 0, j)),
        compiler_params=_params(2),
        name="ada_modulation",
    )(c_all, ada_w, _rows(ada_b))


def _mlp_kernel(x_ref, mod_ref, g_ref, wu_ref, wd_ref, fn_ref, *rest, ff_chunk, final):
    if len(rest) > 1:
        wu_next_ref, wd_next_ref, o_ref, wu_next_out, wd_next_out = rest
        wu_next_out[...] = wu_next_ref[...].astype(BF16)
        wd_next_out[...] = wd_next_ref[...].astype(BF16)
    else:
        o_ref, = rest
    x = x_ref[...]
    d = x.shape[-1]
    shift, scale, gate = (mod_ref[:, k * d:(k + 1) * d] for k in (3, 4, 5))
    h = _rms_mod(x, g_ref[...], scale, shift).astype(BF16)
    acc = jnp.zeros_like(x)
    for j in range(wu_ref.shape[1] // ff_chunk):
        u = jnp.maximum(_dot(h, wu_ref[:, j * ff_chunk:(j + 1) * ff_chunk]), 0.0)
        acc = acc + _dot((u * u).astype(BF16), wd_ref[j * ff_chunk:(j + 1) * ff_chunk, :])
    y = x + gate * acc
    if final:
        y = y * lax.rsqrt(jnp.mean(y * y, axis=-1, keepdims=True) + EPS) * fn_ref[...]
    o_ref[...] = y


def _mlp_layer(x, mod, w, layer, *, final, tm, ff_chunk):
    bt, s, d = x.shape
    if w["mlp_bf16"][layer] is None:
        w["mlp_bf16"][layer] = tuple(p[layer].astype(BF16) for p in w["mlp_f32"])
    w_up, w_down = w["mlp_bf16"][layer]
    f = w_up.shape[-1]
    ff_chunk = min(ff_chunk, f)
    assert s % tm == 0 and f % ff_chunk == 0
    tiles = s // tm
    steps = bt * tiles
    xf = x.reshape(bt * s, d)
    in_specs = [
        pl.BlockSpec((tm, d), lambda i: (i, 0)),
        pl.BlockSpec((None, 1, N_ADA * d), lambda i: (i // tiles, 0, 0)),
        _layer_slice((1, d), layer),
        _resident((d, f)),
        _resident((f, d)),
        _resident((1, d)),
    ]
    args = [xf, mod, w["norm_mlp"], w_up, w_down, w["final_norm"]]
    out_shape = [jax.ShapeDtypeStruct(xf.shape, F32)]
    out_specs = [pl.BlockSpec((tm, d), lambda i: (i, 0))]
    nxt = layer + 1
    cast_next = (nxt < len(w["mlp_bf16"]) and w["mlp_bf16"][nxt] is None
                 and d % (steps * BF16_ROWS) == 0 and f % (steps * BF16_ROWS) == 0)
    if cast_next:
        in_specs += [pl.BlockSpec((None, d // steps, f), lambda i: (nxt, i, 0)),
                     pl.BlockSpec((None, f // steps, d), lambda i: (nxt, i, 0))]
        args += list(w["mlp_f32"])
        out_shape += [jax.ShapeDtypeStruct((d, f), BF16), jax.ShapeDtypeStruct((f, d), BF16)]
        out_specs += [pl.BlockSpec((d // steps, f), lambda i: (i, 0)),
                      pl.BlockSpec((f // steps, d), lambda i: (i, 0))]
    out = pl.pallas_call(
        functools.partial(_mlp_kernel, ff_chunk=ff_chunk, final=final),
        out_shape=out_shape,
        grid=(steps,),
        in_specs=in_specs,
        out_specs=out_specs,
        compiler_params=_params(1),
        name="mlp_final" if final else "mlp",
    )(*args)
    if cast_next:
        w["mlp_bf16"][nxt] = (out[1], out[2])
    return out[0].reshape(bt, s, d)


HALO = BF16_ROWS


def _halo_specs(tm, d, n_rows):
    per = tm // HALO
    last = n_rows // HALO - 1
    return [
        pl.BlockSpec((HALO, d), lambda i: (jnp.maximum(i * per - 1, 0), 0)),
        pl.BlockSpec((tm, d), lambda i: (i, 0)),
        pl.BlockSpec((HALO, d), lambda i: (jnp.minimum((i + 1) * per, last), 0)),
    ]


def _fill_hext(hext, xp_ref, x_ref, xn_ref, gain, scale, shift, tm):
    hext[0:HALO, :] = _rms_mod(xp_ref[...], gain, scale, shift).astype(BF16)
    hext[HALO:HALO + tm, :] = _rms_mod(x_ref[...], gain, scale, shift).astype(BF16)
    hext[HALO + tm:, :] = _rms_mod(xn_ref[...], gain, scale, shift).astype(BF16)


def _zero_outside_sequence(u, tm, tiles):
    t = pl.program_id(0) % tiles
    before = jnp.where(t == 0, 0.0, u[:HALO])
    after = jnp.where(t == tiles - 1, 0.0, u[HALO + tm:])
    return jnp.concatenate([before, u[HALO:HALO + tm], after], axis=0)


def _sconv_kernel(xp_ref, x_ref, xn_ref, mod_ref, g_ref, win_ref, cw_ref, wout_ref, o_ref, hext,
                  *, tm, tiles):
    d = x_ref.shape[-1]
    n = tm + 2 * HALO
    shift, scale, gate = (mod_ref[:, k * d:(k + 1) * d] for k in (0, 1, 2))
    _fill_hext(hext, xp_ref, x_ref, xn_ref, g_ref[...], scale, shift, tm)
    cv = _dot(hext[...], win_ref[:, d:3 * d])
    u = _zero_outside_sequence(cv[:, :d] * cv[:, d:], tm, tiles)
    prev = pltpu.roll(u, 1, axis=0)[HALO:HALO + tm]
    nxt = pltpu.roll(u, n - 1, axis=0)[HALO:HALO + tm]
    y = cw_ref[0:1, :] * prev + cw_ref[1:2, :] * u[HALO:HALO + tm] + cw_ref[2:3, :] * nxt
    b = _dot(hext[HALO:HALO + tm, :], win_ref[:, 0:d])
    m = _dot((b * y).astype(BF16), wout_ref[...])
    o_ref[...] = x_ref[...] + gate * m


def _sconv_layer(x, mod, w, layer, j, *, tm):
    bt, s, d = x.shape
    assert s % tm == 0 and tm % HALO == 0 and w["a_conv_w"].shape[1] == 3
    tiles = s // tm
    xf = x.reshape(bt * s, d)
    out = pl.pallas_call(
        functools.partial(_sconv_kernel, tm=tm, tiles=tiles),
        out_shape=jax.ShapeDtypeStruct(xf.shape, F32),
        grid=(bt * tiles,),
        in_specs=_halo_specs(tm, d, bt * s) + [
            pl.BlockSpec((None, 1, N_ADA * d), lambda i: (i // tiles, 0, 0)),
            _layer_slice((1, d), layer),
            _layer_slice((d, 3 * d), j),
            _layer_slice((3, d), j),
            _layer_slice((d, d), j),
        ],
        out_specs=pl.BlockSpec((tm, d), lambda i: (i, 0)),
        scratch_shapes=[pltpu.VMEM((tm + 2 * HALO, d), BF16)],
        compiler_params=_params(1),
        name="short_conv",
    )(xf, xf, xf, mod, w["norm_mix"], w["a_w_in"], w["a_conv_w"], w["a_w_out"])
    return out.reshape(bt, s, d)


CONV_ROWS = 4 * BF16_ROWS


def _conformer_kernel(xp_ref, x_ref, xn_ref, mod_ref, g_ref, w1_ref, b1_ref, dw_ref, dwb_ref,
                      lng_ref, lnb_ref, w2_ref, b2_ref, o_ref, hext, even, odd, wtap, conv,
                      *, tm, tiles, width):
    d = x_ref.shape[-1]
    n = tm + 2 * HALO
    pad = width // 2
    lane_tiles = d // LANES

    @pl.when(pl.program_id(0) == 0)
    def _():
        for k in range(width):
            for j in range(lane_tiles):
                tap = dw_ref[k:k + 1, j * LANES:(j + 1) * LANES]
                wtap[k, j] = jnp.broadcast_to(tap, (BF16_ROWS, LANES)).astype(BF16)

    shift, scale, gate = (mod_ref[:, k * d:(k + 1) * d] for k in (0, 1, 2))
    _fill_hext(hext, xp_ref, x_ref, xn_ref, g_ref[...], scale, shift, tm)
    ag = _dot(hext[...], w1_ref[...]) + b1_ref[...]
    u = _zero_outside_sequence(ag[:, :d] * jax.nn.sigmoid(ag[:, d:]), tm, tiles)
    pe = pltpu.bitcast(u.astype(BF16), U32)
    po = pltpu.bitcast(pltpu.roll(u, n - 1, axis=0).astype(BF16), U32)
    for j in range(lane_tiles):
        even[j] = pe[:, j * LANES:(j + 1) * LANES]
        odd[j] = po[:, j * LANES:(j + 1) * LANES]

    half = BF16_ROWS // 2

    def chunk(ci, carry):
        word0 = pl.multiple_of(ci * (CONV_ROWS // 2), CONV_ROWS // 2)
        row0 = pl.multiple_of(ci * CONV_ROWS, CONV_ROWS)
        for j in range(lane_tiles):
            acc = [jnp.zeros((BF16_ROWS, LANES), F32) for _ in range(CONV_ROWS // BF16_ROWS)]
            for k in range(width):
                off = HALO - pad + k
                src = odd if off % 2 else even
                tap = wtap[k, j].astype(F32)
                for a in range(len(acc)):
                    words = src[j, pl.ds(word0 + off // 2 + a * half, half), :]
                    acc[a] = acc[a] + tap * pltpu.bitcast(words, BF16).astype(F32)
            bias = dwb_ref[:, j * LANES:(j + 1) * LANES]
            for a in range(len(acc)):
                conv[pl.ds(row0 + a * BF16_ROWS, BF16_ROWS), j * LANES:(j + 1) * LANES] = acc[a] + bias
        return carry

    lax.fori_loop(0, tm // CONV_ROWS, chunk, 0)
    v = conv[...]
    mu = jnp.mean(v, axis=-1, keepdims=True)
    vc = v - mu
    ln = vc * lax.rsqrt(jnp.mean(vc * vc, axis=-1, keepdims=True) + EPS) * lng_ref[...] + lnb_ref[...]
    act = (ln * jax.nn.sigmoid(ln)).astype(BF16)
    m = _dot(act, w2_ref[...]) + b2_ref[...]
    o_ref[...] = x_ref[...] + gate * m


def _conformer_layer(x, mod, w, layer, j, *, tm):
    bt, s, d = x.shape
    width = w["c_dw_w"].shape[1]
    assert s % tm == 0 and tm % CONV_ROWS == 0 and width // 2 < HALO and d % LANES == 0
    tiles = s // tm
    n = tm + 2 * HALO
    xf = x.reshape(bt * s, d)
    out = pl.pallas_call(
        functools.partial(_conformer_kernel, tm=tm, tiles=tiles, width=width),
        out_shape=jax.ShapeDtypeStruct(xf.shape, F32),
        grid=(bt * tiles,),
        in_specs=_halo_specs(tm, d, bt * s) + [
            pl.BlockSpec((None, 1, N_ADA * d), lambda i: (i // tiles, 0, 0)),
            _layer_slice((1, d), layer),
            _layer_slice((d, 2 * d), j),
            _layer_slice((1, 2 * d), j),
            _layer_slice((width, d), j),
            _layer_slice((1, d), j),
            _layer_slice((1, d), j),
            _layer_slice((1, d), j),
            _layer_slice((d, d), j),
            _layer_slice((1, d), j),
        ],
        out_specs=pl.BlockSpec((tm, d), lambda i: (i, 0)),
        scratch_shapes=[
            pltpu.VMEM((n, d), BF16),
            pltpu.VMEM((d // LANES, n // 2, LANES), U32),
            pltpu.VMEM((d // LANES, n // 2, LANES), U32),
            pltpu.VMEM((width, d // LANES, BF16_ROWS, LANES), BF16),
            pltpu.VMEM((tm, d), F32),
        ],
        compiler_params=_params(1),
        name="conformer",
    )(xf, xf, xf, mod, w["norm_mix"], w["c_w_pw1"], w["c_b_pw1"], w["c_dw_w"], w["c_dw_b"],
      w["c_ln_g"], w["c_ln_b"], w["c_w_pw2"], w["c_b_pw2"])
    return out.reshape(bt, s, d)


PAIR = 2 * SUBLANES
FFT_TABLE_BYTES = 8 * 1024 * 1024


def _fft_factors(s):
    known = {4096: (16, 16, 16), 16384: (32, 32, 16)}
    if s in known:
        return known[s]
    best = None
    for a in (8, 16, 32, 64):
        for c in (8, 16, 32, 64):
            if s % (a * c) == 0:
                b = s // (a * c)
                cost = max(a, 32) + 2 * max(b, 16) + 2 * max(c, 16)
                if best is None or cost < best[0]:
                    best = (cost, (a, b, c))
    assert best is not None, f"sequence length {s} needs two factors that are multiples of 8"
    return best[1]


def _largest_divisor(n, limit):
    k = max(1, min(n, limit))
    while n % k:
        k -= 1
    return k


def _cos_sin(phase, period):
    ang = phase.astype(F32) * F32(2.0 * math.pi / period)
    return jnp.cos(ang), jnp.sin(ang)


def _stage_tables(n, part_major_rows):
    size = PAIR * n
    r = lax.broadcasted_iota(jnp.int32, (size, size), 0)
    q = lax.broadcasted_iota(jnp.int32, (size, size), 1)
    m, pi = q // PAIR, (q // SUBLANES) % 2
    if part_major_rows:
        po, k = r // (SUBLANES * n), (r // SUBLANES) % n
    else:
        po, k = (r // SUBLANES) % 2, r // PAIR
    co, si = _cos_sin((k * m) % n, n)
    sign = (pi - po).astype(F32)
    same = (r % SUBLANES) == (q % SUBLANES)
    p = jnp.where(same, jnp.where(po == pi, co, sign * si), 0.0)
    qq = jnp.where(same, jnp.where(po == pi, -si, sign * co), 0.0)
    return p, qq


def _fft_tables(s, dg):
    a_n, b_n, c_n = _fft_factors(s)
    i32 = jnp.int32
    ar = functools.partial(jnp.arange, dtype=i32)

    r = lax.broadcasted_iota(i32, (PAIR * a_n, SUBLANES * a_n), 0)
    q = lax.broadcasted_iota(i32, (PAIR * a_n, SUBLANES * a_n), 1)
    ka = (r // (SUBLANES * PAIR)) * SUBLANES + r % SUBLANES
    co, si = _cos_sin((ka * (q // SUBLANES)) % a_n, a_n)
    m1 = jnp.where(((r // PAIR) % SUBLANES) == (q % SUBLANES),
                   jnp.where((r // SUBLANES) % 2 == 0, co, -si), 0.0).astype(BF16)

    pq2 = _stage_tables(b_n, False)
    pq3 = _stage_tables(c_n, True)

    ka = ar(a_n // 8)[:, None, None, None] * 8 + ar(8)[None, None, None, :]
    both = jnp.zeros((1, 1, 2, 1), i32)
    tw1 = _cos_sin((ar(b_n)[None, :, None, None] * ka + both) % (a_n * b_n), a_n * b_n)
    tw1 = [t.reshape(a_n // 8, 1, PAIR * b_n) for t in tw1]
    kk = ka[None] + a_n * ar(b_n)[:, None, None, None, None]
    tw2 = _cos_sin((ar(c_n)[None, None, :, None, None] * kk + both[None]) % s, s)
    tw2 = [t.reshape(b_n, a_n // 8, 1, PAIR * c_n) for t in tw2]

    co, si = _cos_sin((ar(dg)[:, None] * ar(dg)[None, :]) % dg, dg)
    norm = F32(1.0 / math.sqrt(s * dg))
    return m1, pq2, pq3, tw1, tw2, (co * norm).astype(BF16), (si * norm).astype(BF16)


def _fft1_kernel(x_ref, mod_ref, g_ref, m_ref, o_ref):
    a_n, bg_n, c_n, d = x_ref.shape
    shift, scale = (mod_ref[:, k * d:(k + 1) * d] for k in (0, 1))
    for bg in range(bg_n):
        for ch in range(c_n // SUBLANES):
            rows = slice(ch * SUBLANES, (ch + 1) * SUBLANES)
            xs = x_ref[:, bg, rows, :].reshape(a_n * SUBLANES, d)
            h = _rms_mod(xs, g_ref[...], scale, shift).astype(BF16)
            y = _dot(m_ref[...], h).astype(BF16)
            o_ref[:, rows, bg] = y.reshape(a_n // SUBLANES, SUBLANES, PAIR, d)


def _fft2_kernel(t_ref, p_ref, q_ref, tc_ref, ts_ref, o_ref, mat):
    cg_n, b_n, _, d = t_ref.shape

    @pl.when(pl.program_id(2) == 0)
    def _():
        mat[...] = (p_ref[...] * tc_ref[...] + q_ref[...] * ts_ref[...]).astype(BF16)

    for cg in range(cg_n):
        y = _dot(mat[...], t_ref[cg].reshape(b_n * PAIR, d)).astype(BF16)
        o_ref[:, cg] = y.reshape(b_n, PAIR, d)


def _fft12_kernel(x_ref, mod_ref, g_ref, m_ref, p_ref, q_ref, tc_ref, ts_ref, o_ref, t1, mat):
    a_n, b_n, _, d = x_ref.shape
    ah_n = a_n // SUBLANES

    @pl.when((pl.program_id(0) == 0) & (pl.program_id(1) == 0))
    def _():
        for h in range(ah_n):
            mat[h] = (p_ref[...] * tc_ref[h] + q_ref[...] * ts_ref[h]).astype(BF16)

    shift, scale = (mod_ref[:, k * d:(k + 1) * d] for k in (0, 1))
    for b in range(b_n):
        hb = _rms_mod(x_ref[:, b].reshape(a_n * SUBLANES, d), g_ref[...], scale, shift).astype(BF16)
        y = _dot(m_ref[...], hb).astype(BF16)
        t1[:, :, b] = y.reshape(ah_n, SUBLANES, PAIR, d)
    for h in range(ah_n):
        for cl in range(SUBLANES):
            y = _dot(mat[h], t1[h, cl].reshape(b_n * PAIR, d)).astype(BF16)
            o_ref[:, h, cl] = y.reshape(b_n, PAIR, d)


def _fft3_kernel(t_ref, x_ref, mod_ref, p_ref, q_ref, tc_ref, ts_ref, cc_ref, sc_ref, w_ref, b_ref,
                 o_ref, yr, yi, *, groups):
    kb_n, ah_n, c_n = t_ref.shape[0], t_ref.shape[1], t_ref.shape[2]
    d = t_ref.shape[-1]
    rows = c_n * SUBLANES
    gate = mod_ref[:, 2 * d:3 * d]
    for j in range(kb_n):
        for h in range(ah_n):
            mat = (p_ref[...] * tc_ref[j, h] + q_ref[...] * ts_ref[j, h]).astype(BF16)
            y = _dot(mat, t_ref[j, h].reshape(c_n * PAIR, d)).astype(BF16)
            r0 = (j * ah_n + h) * rows
            yr[r0:r0 + rows, :] = y[:rows]
            yi[r0:r0 + rows, :] = y[rows:]
    dg = d // groups
    f = jnp.concatenate(
        [_dot(yr[:, g * dg:(g + 1) * dg], cc_ref[...]) + _dot(yi[:, g * dg:(g + 1) * dg], sc_ref[...])
         for g in range(groups)], axis=1)
    m = _dot(f.astype(BF16), w_ref[...]) + b_ref[...]
    for j in range(kb_n):
        for h in range(ah_n):
            r0 = (j * ah_n + h) * rows
            o_ref[:, j, h] = x_ref[:, j, h] + (gate * m[r0:r0 + rows]).reshape(c_n, SUBLANES, d)


def _fourier_layer(x, mod, w, layer, j, *, step_rows=1024):
    bt, s, d = x.shape
    a_n, b_n, c_n = _fft_factors(s)
    assert a_n * b_n * c_n == s and a_n % 8 == 0 and c_n % 8 == 0 and d % FNET_GROUPS == 0
    ah_n = a_n // 8
    dg = d // FNET_GROUPS
    m1, pq2, pq3, tw1, tw2, cc, sc = _fft_tables(s, dg)
    mod_spec = pl.BlockSpec((None, 1, N_ADA * d), lambda *i: (i[0], 0, 0))

    slab_bytes = a_n * b_n * SUBLANES * d * 4
    if 5 * slab_bytes + FFT_TABLE_BYTES <= VMEM_LIMIT_BYTES:
        ch_n = c_n // SUBLANES
        t2 = pl.pallas_call(
            _fft12_kernel,
            out_shape=jax.ShapeDtypeStruct((bt, b_n, ah_n, ch_n, SUBLANES, PAIR, d), BF16),
            grid=(bt, ch_n),
            in_specs=[
                pl.BlockSpec((None, a_n, b_n, None, SUBLANES, d), lambda i, c: (i, 0, 0, c, 0, 0)),
                mod_spec,
                _layer_slice((1, d), layer),
                _resident(m1.shape),
                _resident(pq2[0].shape), _resident(pq2[1].shape),
                _resident(tw1[0].shape), _resident(tw1[1].shape),
            ],
            out_specs=pl.BlockSpec((None, b_n, ah_n, None, SUBLANES, PAIR, d),
                                   lambda i, c: (i, 0, 0, c, 0, 0, 0)),
            scratch_shapes=[pltpu.VMEM((ah_n, SUBLANES, b_n, PAIR, d), BF16),
                            pltpu.VMEM((ah_n, PAIR * b_n, PAIR * b_n), BF16)],
            compiler_params=_params(2),
            name="fourier_seq12",
        )(x.reshape(bt, a_n, b_n, ch_n, SUBLANES, d), mod, w["norm_mix"], m1, *pq2, *tw1)
        t2 = t2.reshape(bt, b_n, ah_n, c_n, PAIR, d)
    else:
        t2 = _fourier_stages_1_2(x, mod, w, layer, (a_n, b_n, c_n), m1, pq2, tw1, mod_spec, step_rows)
    return _fourier_stage_3(x, mod, w, j, (a_n, b_n, c_n), t2, pq3, tw2, cc, sc, mod_spec, step_rows)


def _fourier_stages_1_2(x, mod, w, layer, factors, m1, pq2, tw1, mod_spec, step_rows):
    bt, s, d = x.shape
    a_n, b_n, c_n = factors
    ah_n = a_n // 8

    bg = _largest_divisor(b_n, step_rows // (a_n * c_n))
    t1 = pl.pallas_call(
        _fft1_kernel,
        out_shape=jax.ShapeDtypeStruct((bt, ah_n, c_n, b_n, PAIR, d), BF16),
        grid=(bt, b_n // bg),
        in_specs=[
            pl.BlockSpec((None, a_n, bg, c_n, d), lambda i, q: (i, 0, q, 0, 0)),
            mod_spec,
            _layer_slice((1, d), layer),
            _resident(m1.shape),
        ],
        out_specs=pl.BlockSpec((None, ah_n, c_n, bg, PAIR, d), lambda i, q: (i, 0, 0, q, 0, 0)),
        compiler_params=_params(2),
        name="fourier_seq1",
    )(x.reshape(bt, a_n, b_n, c_n, d), mod, w["norm_mix"], m1)

    cg = _largest_divisor(c_n, step_rows // (b_n * SUBLANES))
    tw1_spec = pl.BlockSpec((None, 1, PAIR * b_n), lambda i, h, c: (h, 0, 0))
    return pl.pallas_call(
        _fft2_kernel,
        out_shape=jax.ShapeDtypeStruct((bt, b_n, ah_n, c_n, PAIR, d), BF16),
        grid=(bt, ah_n, c_n // cg),
        in_specs=[
            pl.BlockSpec((None, None, cg, b_n, PAIR, d), lambda i, h, c: (i, h, c, 0, 0, 0)),
            _resident(pq2[0].shape), _resident(pq2[1].shape), tw1_spec, tw1_spec,
        ],
        out_specs=pl.BlockSpec((None, b_n, None, cg, PAIR, d), lambda i, h, c: (i, 0, h, c, 0, 0)),
        scratch_shapes=[pltpu.VMEM((PAIR * b_n, PAIR * b_n), BF16)],
        compiler_params=_params(3),
        name="fourier_seq2",
    )(t1, *pq2, *tw1)


def _fourier_stage_3(x, mod, w, j, factors, t2, pq3, tw2, cc, sc, mod_spec, step_rows):
    bt, s, d = x.shape
    a_n, b_n, c_n = factors
    ah_n = a_n // 8
    dg = d // FNET_GROUPS

    kb_blk = _largest_divisor(b_n, step_rows // (ah_n * c_n * SUBLANES))
    rows = kb_blk * ah_n * c_n * SUBLANES
    xo_spec = pl.BlockSpec((None, c_n, kb_blk, ah_n, SUBLANES, d), lambda i, k: (i, 0, k, 0, 0, 0))
    tw2_spec = pl.BlockSpec((kb_blk, ah_n, 1, PAIR * c_n), lambda i, k: (k, 0, 0, 0))
    out = pl.pallas_call(
        functools.partial(_fft3_kernel, groups=FNET_GROUPS),
        out_shape=jax.ShapeDtypeStruct((bt, c_n, b_n, ah_n, SUBLANES, d), F32),
        grid=(bt, b_n // kb_blk),
        in_specs=[
            pl.BlockSpec((None, kb_blk, ah_n, c_n, PAIR, d), lambda i, k: (i, k, 0, 0, 0, 0)),
            xo_spec, mod_spec,
            _resident(pq3[0].shape), _resident(pq3[1].shape), tw2_spec, tw2_spec,
            _resident((dg, dg)), _resident((dg, dg)),
            _layer_slice((d, d), j),
            _layer_slice((1, d), j),
        ],
        out_specs=xo_spec,
        scratch_shapes=[pltpu.VMEM((rows, d), BF16), pltpu.VMEM((rows, d), BF16)],
        compiler_params=_params(2),
        name="fourier_seq3",
    )(t2, x.reshape(bt, c_n, b_n, ah_n, SUBLANES, d), mod, *pq3, *tw2, cc, sc, w["b_w_out"], w["b_b_out"])
    return out.reshape(bt, s, d)


def _tile_rows(s, target):
    tm = min(s, target)
    while s % tm:
        tm //= 2
    return tm


def _trunk(x, mods, w, *, mlp_tm=1024, conv_tm=1024, ff_chunk=1024):
    depth = w["norm_mix"].shape[0]
    s = x.shape[1]
    for i in range(depth):
        kind, j = i % N_MIXERS, i // N_MIXERS
        if kind == 0:
            x = _sconv_layer(x, mods[i], w, i, j, tm=_tile_rows(s, conv_tm))
        elif kind == 1:
            x = _fourier_layer(x, mods[i], w, i, j)
        else:
            x = _conformer_layer(x, mods[i], w, i, j, tm=_tile_rows(s, conv_tm))
        x = _mlp_layer(x, mods[i], w, i, final=(i == depth - 1), tm=_tile_rows(s, mlp_tm),
                       ff_chunk=ff_chunk)
    return x


def kernel(x_prompt, x_sample, c_prompt, c_sample, ada_w, ada_b, norm_mix, norm_mlp, a_w_in, a_conv_w, a_w_out, b_w_out, b_b_out, c_w_pw1, c_b_pw1, c_dw_w, c_dw_b, c_ln_g, c_ln_b, c_w_pw2, c_b_pw2, mlp_w_up, mlp_w_down, final_norm):
    d = x_prompt.shape[-1]
    n_p, n_s = c_prompt.shape[0], c_sample.shape[0]
    pad_rows = -(n_p + n_s) % SUBLANES
    c_all = jnp.concatenate([c_prompt, c_sample, jnp.zeros((pad_rows, d), F32)], axis=0)
    mod = _ada_modulation(c_all, ada_w, ada_b)
    mods_p = mod[:, :n_p, None, :]
    mods_s = mod[:, n_p:n_p + n_s, None, :]
    w = dict(
        norm_mix=_rows(norm_mix), norm_mlp=_rows(norm_mlp), final_norm=final_norm.reshape(1, d),
        a_w_in=a_w_in.astype(BF16), a_conv_w=a_conv_w, a_w_out=a_w_out.astype(BF16),
        b_w_out=b_w_out.astype(BF16), b_b_out=_rows(b_b_out),
        c_w_pw1=c_w_pw1.astype(BF16), c_b_pw1=_rows(c_b_pw1), c_dw_w=c_dw_w, c_dw_b=_rows(c_dw_b),
        c_ln_g=_rows(c_ln_g), c_ln_b=_rows(c_ln_b), c_w_pw2=c_w_pw2.astype(BF16), c_b_pw2=_rows(c_b_pw2),
        mlp_f32=(mlp_w_up, mlp_w_down), mlp_bf16=[None] * mlp_w_up.shape[0],
    )
    return _trunk(x_prompt, mods_p, w), _trunk(x_sample, mods_s, w)
```

```python
import functools
import math

import jax
import jax.numpy as jnp
from jax import lax
from jax.experimental import pallas as pl
from jax.experimental.pallas import tpu as pltpu

EPS = 1e-6
N_ADA = 6
N_MIXERS = 3
FNET_GROUPS = 4

SUBLANES = 8
LANES = 128
BF16_ROWS = 16
VMEM_LIMIT_BYTES = 56 * 1024 * 1024

F32 = jnp.float32
BF16 = jnp.bfloat16
U32 = jnp.uint32


def _params(n_axes):
    return pltpu.CompilerParams(
        dimension_semantics=("arbitrary",) * n_axes, vmem_limit_bytes=VMEM_LIMIT_BYTES)


def _resident(shape):
    zeros = (0,) * len(shape)
    return pl.BlockSpec(shape, lambda *_: zeros, pipeline_mode=pl.Buffered(1))


def _layer_slice(shape, layer):
    index = (layer,) + (0,) * len(shape)
    return pl.BlockSpec((None,) + shape, lambda *_: index, pipeline_mode=pl.Buffered(1))


def _rows(p):
    return p.reshape(p.shape[0], 1, p.shape[1])


def _rms_mod(x, gain, scale, shift):
    ms = jnp.mean(x * x, axis=-1, keepdims=True)
    return (x * lax.rsqrt(ms + EPS)) * gain * (1.0 + scale) + shift


def _dot(a, b):
    return jnp.dot(a, b, preferred_element_type=F32)


def _ada_kernel(c_ref, w_ref, b_ref, o_ref):
    c = c_ref[...]
    c_act = (c * jax.nn.sigmoid(c)).astype(BF16)
    o_ref[...] = _dot(c_act, w_ref[...].astype(BF16)) + b_ref[...]


def _ada_modulation(c_all, ada_w, ada_b, col_block=1536):
    depth, d, n = ada_w.shape
    rows = c_all.shape[0]
    assert n % col_block == 0
    return pl.pallas_call(
        _ada_kernel,
        out_shape=jax.ShapeDtypeStruct((depth, rows, n), F32),
        grid=(depth, n // col_block),
        in_specs=[
            pl.BlockSpec((rows, d), lambda l, j: (0, 0)),
            pl.BlockSpec((None, d, col_block), lambda l, j: (l, 0, j)),
            pl.BlockSpec((None, 1, col_block), lambda l, j: (l, 0, j)),
        ],
        out_specs=pl.BlockSpec((None, rows, col_block), lambda l, j: (l, 0, j)),
        compiler_params=_params(2),
        name="ada_modulation",
    )(c_all, ada_w, _rows(ada_b))


def _mlp_kernel(x_ref, mod_ref, g_ref, wu_ref, wd_ref, fn_ref, *rest, ff_chunk, final):
    if len(rest) > 1:
        wu_next_ref, wd_next_ref, o_ref, wu_next_out, wd_next_out = rest
        wu_next_out[...] = wu_next_ref[...].astype(BF16)
        wd_next_out[...] = wd_next_ref[...].astype(BF16)
    else:
        o_ref, = rest
    x = x_ref[...]
    d = x.shape[-1]
    shift, scale, gate = (mod_ref[:, k * d:(k + 1) * d] for k in (3, 4, 5))
    h = _rms_mod(x, g_ref[...], scale, shift).astype(BF16)
    acc = jnp.zeros_like(x)
    for j in range(wu_ref.shape[1] // ff_chunk):
        u = jnp.maximum(_dot(h, wu_ref[:, j * ff_chunk:(j + 1) * ff_chunk]), 0.0)
        acc = acc + _dot((u * u).astype(BF16), wd_ref[j * ff_chunk:(j + 1) * ff_chunk, :])
    y = x + gate * acc
    if final:
        y = y * lax.rsqrt(jnp.mean(y * y, axis=-1, keepdims=True) + EPS) * fn_ref[...]
    o_ref[...] = y


def _mlp_layer(x, mod, w, layer, *, final, tm, ff_chunk):
    bt, s, d = x.shape
    if w["mlp_bf16"][layer] is None:
        w["mlp_bf16"][layer] = tuple(p[layer].astype(BF16) for p in w["mlp_f32"])
    w_up, w_down = w["mlp_bf16"][layer]
    f = w_up.shape[-1]
    ff_chunk = min(ff_chunk, f)
    assert s % tm == 0 and f % ff_chunk == 0
    tiles = s // tm
    steps = bt * tiles
    xf = x.reshape(bt * s, d)
    in_specs = [
        pl.BlockSpec((tm, d), lambda i: (i, 0)),
        pl.BlockSpec((None, 1, N_ADA * d), lambda i: (i // tiles, 0, 0)),
        _layer_slice((1, d), layer),
        _resident((d, f)),
        _resident((f, d)),
        _resident((1, d)),
    ]
    args = [xf, mod, w["norm_mlp"], w_up, w_down, w["final_norm"]]
    out_shape = [jax.ShapeDtypeStruct(xf.shape, F32)]
    out_specs = [pl.BlockSpec((tm, d), lambda i: (i, 0))]
    nxt = layer + 1
    cast_next = (nxt < len(w["mlp_bf16"]) and w["mlp_bf16"][nxt] is None
                 and d % (steps * BF16_ROWS) == 0 and f % (steps * BF16_ROWS) == 0)
    if cast_next:
        in_specs += [pl.BlockSpec((None, d // steps, f), lambda i: (nxt, i, 0)),
                     pl.BlockSpec((None, f // steps, d), lambda i: (nxt, i, 0))]
        args += list(w["mlp_f32"])
        out_shape += [jax.ShapeDtypeStruct((d, f), BF16), jax.ShapeDtypeStruct((f, d), BF16)]
        out_specs += [pl.BlockSpec((d // steps, f), lambda i: (i, 0)),
                      pl.BlockSpec((f // steps, d), lambda i: (i, 0))]
    out = pl.pallas_call(
        functools.partial(_mlp_kernel, ff_chunk=ff_chunk, final=final),
        out_shape=out_shape,
        grid=(steps,),
        in_specs=in_specs,
        out_specs=out_specs,
        compiler_params=_params(1),
        name="mlp_final" if final else "mlp",
    )(*args)
    if cast_next:
        w["mlp_bf16"][nxt] = (out[1], out[2])
    return out[0].reshape(bt, s, d)


CHUNK_LANES = 2 * LANES
HALO = BF16_ROWS


def _halo_specs(tm, d, n_rows):
    per = tm // HALO
    last = n_rows // HALO - 1
    return [
        pl.BlockSpec((HALO, d), lambda i: (jnp.maximum(i * per - 1, 0), 0)),
        pl.BlockSpec((tm, d), lambda i: (i, 0)),
        pl.BlockSpec((HALO, d), lambda i: (jnp.minimum((i + 1) * per, last), 0)),
    ]


def _fill_hext(hext, xp_ref, x_ref, xn_ref, gain, scale, shift, tm):
    hext[0:HALO, :] = _rms_mod(xp_ref[...], gain, scale, shift).astype(BF16)
    hext[HALO:HALO + tm, :] = _rms_mod(x_ref[...], gain, scale, shift).astype(BF16)
    hext[HALO + tm:, :] = _rms_mod(xn_ref[...], gain, scale, shift).astype(BF16)


def _zero_outside_sequence(u, tm, tiles):
    t = pl.program_id(0) % tiles
    before = jnp.where(t == 0, 0.0, u[:HALO])
    after = jnp.where(t == tiles - 1, 0.0, u[HALO + tm:])
    return jnp.concatenate([before, u[HALO:HALO + tm], after], axis=0)


def _sconv_kernel(xp_ref, x_ref, xn_ref, mod_ref, g_ref, win_ref, cw_ref, wout_ref, o_ref, hext,
                  *, tm, tiles):
    d = x_ref.shape[-1]
    n = tm + 2 * HALO
    shift, scale, gate = (mod_ref[:, k * d:(k + 1) * d] for k in (0, 1, 2))
    _fill_hext(hext, xp_ref, x_ref, xn_ref, g_ref[...], scale, shift, tm)
    cv = _dot(hext[...], win_ref[:, d:3 * d])
    u = _zero_outside_sequence(cv[:, :d] * cv[:, d:], tm, tiles)
    prev = pltpu.roll(u, 1, axis=0)[HALO:HALO + tm]
    nxt = pltpu.roll(u, n - 1, axis=0)[HALO:HALO + tm]
    y = cw_ref[0:1, :] * prev + cw_ref[1:2, :] * u[HALO:HALO + tm] + cw_ref[2:3, :] * nxt
    b = _dot(hext[HALO:HALO + tm, :], win_ref[:, 0:d])
    m = _dot((b * y).astype(BF16), wout_ref[...])
    o_ref[...] = x_ref[...] + gate * m


def _sconv_layer(x, mod, w, layer, j, *, tm):
    bt, s, d = x.shape
    assert s % tm == 0 and tm % HALO == 0 and w["a_conv_w"].shape[1] == 3
    tiles = s // tm
    xf = x.reshape(bt * s, d)
    out = pl.pallas_call(
        functools.partial(_sconv_kernel, tm=tm, tiles=tiles),
        out_shape=jax.ShapeDtypeStruct(xf.shape, F32),
        grid=(bt * tiles,),
        in_specs=_halo_specs(tm, d, bt * s) + [
            pl.BlockSpec((None, 1, N_ADA * d), lambda i: (i // tiles, 0, 0)),
            _layer_slice((1, d), layer),
            _layer_slice((d, 3 * d), j),
            _layer_slice((3, d), j),
            _layer_slice((d, d), j),
        ],
        out_specs=pl.BlockSpec((tm, d), lambda i: (i, 0)),
        scratch_shapes=[pltpu.VMEM((tm + 2 * HALO, d), BF16)],
        compiler_params=_params(1),
        name="short_conv",
    )(xf, xf, xf, mod, w["norm_mix"], w["a_w_in"], w["a_conv_w"], w["a_w_out"])
    return out.reshape(bt, s, d)


CONV_ROWS = 4 * BF16_ROWS


def _conformer_kernel(xp_ref, x_ref, xn_ref, mod_ref, g_ref, w1_ref, b1_ref, dw_ref, dwb_ref,
                      lng_ref, lnb_ref, w2_ref, b2_ref, o_ref, hext, even, odd, wtap, conv,
                      *, tm, tiles, width):
    d = x_ref.shape[-1]
    n = tm + 2 * HALO
    pad = width // 2
    lane_tiles = d // LANES

    @pl.when(pl.program_id(0) == 0)
    def _():
        for k in range(width):
            for j in range(lane_tiles):
                tap = dw_ref[k:k + 1, j * LANES:(j + 1) * LANES]
                wtap[k, j] = jnp.broadcast_to(tap, (BF16_ROWS, LANES)).astype(BF16)

    shift, scale, gate = (mod_ref[:, k * d:(k + 1) * d] for k in (0, 1, 2))
    _fill_hext(hext, xp_ref, x_ref, xn_ref, g_ref[...], scale, shift, tm)
    for c0 in range(0, d, CHUNK_LANES):
        value = _dot(hext[...], w1_ref[:, c0:c0 + CHUNK_LANES]) + b1_ref[:, c0:c0 + CHUNK_LANES]
        glu = _dot(hext[...], w1_ref[:, d + c0:d + c0 + CHUNK_LANES]) + b1_ref[:, d + c0:d + c0 + CHUNK_LANES]
        u = _zero_outside_sequence(value * jax.nn.sigmoid(glu), tm, tiles)
        pe = pltpu.bitcast(u.astype(BF16), U32)
        po = pltpu.bitcast(pltpu.roll(u, n - 1, axis=0).astype(BF16), U32)
        for j in range(c0 // LANES, (c0 + CHUNK_LANES) // LANES):
            l0 = j * LANES - c0
            even[j] = pe[:, l0:l0 + LANES]
            odd[j] = po[:, l0:l0 + LANES]

    half = BF16_ROWS // 2

    def chunk(ci, carry):
        word0 = pl.multiple_of(ci * (CONV_ROWS // 2), CONV_ROWS // 2)
        row0 = pl.multiple_of(ci * CONV_ROWS, CONV_ROWS)
        for j in range(lane_tiles):
            acc = [jnp.zeros((BF16_ROWS, LANES), F32) for _ in range(CONV_ROWS // BF16_ROWS)]
            for k in range(width):
                off = HALO - pad + k
                src = odd if off % 2 else even
                tap = wtap[k, j].astype(F32)
                for a in range(len(acc)):
                    words = src[j, pl.ds(word0 + off // 2 + a * half, half), :]
                    acc[a] = acc[a] + tap * pltpu.bitcast(words, BF16).astype(F32)
            bias = dwb_ref[:, j * LANES:(j + 1) * LANES]
            for a in range(len(acc)):
                conv[pl.ds(row0 + a * BF16_ROWS, BF16_ROWS), j * LANES:(j + 1) * LANES] = acc[a] + bias
        return carry

    lax.fori_loop(0, tm // CONV_ROWS, chunk, 0)
    v = conv[...]
    mu = jnp.mean(v, axis=-1, keepdims=True)
    vc = v - mu
    ln = vc * lax.rsqrt(jnp.mean(vc * vc, axis=-1, keepdims=True) + EPS) * lng_ref[...] + lnb_ref[...]
    act = (ln * jax.nn.sigmoid(ln)).astype(BF16)
    m = _dot(act, w2_ref[...]) + b2_ref[...]
    o_ref[...] = x_ref[...] + gate * m


def _conformer_layer(x, mod, w, layer, j, *, tm):
    bt, s, d = x.shape
    width = w["c_dw_w"].shape[1]
    assert s % tm == 0 and tm % CONV_ROWS == 0 and width // 2 < HALO and d % CHUNK_LANES == 0
    tiles = s // tm
    n = tm + 2 * HALO
    xf = x.reshape(bt * s, d)
    out = pl.pallas_call(
        functools.partial(_conformer_kernel, tm=tm, tiles=tiles, width=width),
        out_shape=jax.ShapeDtypeStruct(xf.shape, F32),
        grid=(bt * tiles,),
        in_specs=_halo_specs(tm, d, bt * s) + [
            pl.BlockSpec((None, 1, N_ADA * d), lambda i: (i // tiles, 0, 0)),
            _layer_slice((1, d), layer),
            _layer_slice((d, 2 * d), j),
            _layer_slice((1, 2 * d), j),
            _layer_slice((width, d), j),
            _layer_slice((1, d), j),
            _layer_slice((1, d), j),
            _layer_slice((1, d), j),
            _layer_slice((d, d), j),
            _layer_slice((1, d), j),
        ],
        out_specs=pl.BlockSpec((tm, d), lambda i: (i, 0)),
        scratch_shapes=[
            pltpu.VMEM((n, d), BF16),
            pltpu.VMEM((d // LANES, n // 2, LANES), U32),
            pltpu.VMEM((d // LANES, n // 2, LANES), U32),
            pltpu.VMEM((width, d // LANES, BF16_ROWS, LANES), BF16),
            pltpu.VMEM((tm, d), F32),
        ],
        compiler_params=_params(1),
        name="conformer",
    )(xf, xf, xf, mod, w["norm_mix"], w["c_w_pw1"], w["c_b_pw1"], w["c_dw_w"], w["c_dw_b"],
      w["c_ln_g"], w["c_ln_b"], w["c_w_pw2"], w["c_b_pw2"])
    return out.reshape(bt, s, d)


PAIR = 2 * SUBLANES
FFT_TABLE_BYTES = 8 * 1024 * 1024


def _fft_factors(s):
    known = {4096: (16, 16, 16), 16384: (32, 32, 16)}
    if s in known:
        return known[s]
    best = None
    for a in (8, 16, 32, 64):
        for c in (8, 16, 32, 64):
            if s % (a * c) == 0:
                b = s // (a * c)
                cost = max(a, 32) + 2 * max(b, 16) + 2 * max(c, 16)
                if best is None or cost < best[0]:
                    best = (cost, (a, b, c))
    assert best is not None, f"sequence length {s} needs two factors that are multiples of 8"
    return best[1]


def _largest_divisor(n, limit):
    k = max(1, min(n, limit))
    while n % k:
        k -= 1
    return k


def _cos_sin(phase, period):
    ang = phase.astype(F32) * F32(2.0 * math.pi / period)
    return jnp.cos(ang), jnp.sin(ang)


def _stage_tables(n, part_major_rows):
    size = PAIR * n
    r = lax.broadcasted_iota(jnp.int32, (size, size), 0)
    q = lax.broadcasted_iota(jnp.int32, (size, size), 1)
    m, pi = q // PAIR, (q // SUBLANES) % 2
    if part_major_rows:
        po, k = r // (SUBLANES * n), (r // SUBLANES) % n
    else:
        po, k = (r // SUBLANES) % 2, r // PAIR
    co, si = _cos_sin((k * m) % n, n)
    sign = (pi - po).astype(F32)
    same = (r % SUBLANES) == (q % SUBLANES)
    p = jnp.where(same, jnp.where(po == pi, co, sign * si), 0.0)
    qq = jnp.where(same, jnp.where(po == pi, -si, sign * co), 0.0)
    return p, qq


def _fft_tables(s, dg):
    a_n, b_n, c_n = _fft_factors(s)
    i32 = jnp.int32
    ar = functools.partial(jnp.arange, dtype=i32)

    r = lax.broadcasted_iota(i32, (PAIR * a_n, SUBLANES * a_n), 0)
    q = lax.broadcasted_iota(i32, (PAIR * a_n, SUBLANES * a_n), 1)
    ka = (r // (SUBLANES * PAIR)) * SUBLANES + r % SUBLANES
    co, si = _cos_sin((ka * (q // SUBLANES)) % a_n, a_n)
    m1 = jnp.where(((r // PAIR) % SUBLANES) == (q % SUBLANES),
                   jnp.where((r // SUBLANES) % 2 == 0, co, -si), 0.0).astype(BF16)

    pq2 = _stage_tables(b_n, False)
    pq3 = _stage_tables(c_n, True)

    ka = ar(a_n // 8)[:, None, None, None] * 8 + ar(8)[None, None, None, :]
    both = jnp.zeros((1, 1, 2, 1), i32)
    tw1 = _cos_sin((ar(b_n)[None, :, None, None] * ka + both) % (a_n * b_n), a_n * b_n)
    tw1 = [t.reshape(a_n // 8, 1, PAIR * b_n) for t in tw1]
    kk = ka[None] + a_n * ar(b_n)[:, None, None, None, None]
    tw2 = _cos_sin((ar(c_n)[None, None, :, None, None] * kk + both[None]) % s, s)
    tw2 = [t.reshape(b_n, a_n // 8, 1, PAIR * c_n) for t in tw2]

    co, si = _cos_sin((ar(dg)[:, None] * ar(dg)[None, :]) % dg, dg)
    norm = F32(1.0 / math.sqrt(s * dg))
    return m1, pq2, pq3, tw1, tw2, (co * norm).astype(BF16), (si * norm).astype(BF16)


def _fft1_kernel(x_ref, mod_ref, g_ref, m_ref, o_ref):
    a_n, bg_n, c_n, d = x_ref.shape
    shift, scale = (mod_ref[:, k * d:(k + 1) * d] for k in (0, 1))
    for bg in range(bg_n):
        for ch in range(c_n // SUBLANES):
            rows = slice(ch * SUBLANES, (ch + 1) * SUBLANES)
            xs = x_ref[:, bg, rows, :].reshape(a_n * SUBLANES, d)
            h = _rms_mod(xs, g_ref[...], scale, shift).astype(BF16)
            y = _dot(m_ref[...], h).astype(BF16)
            o_ref[:, rows, bg] = y.reshape(a_n // SUBLANES, SUBLANES, PAIR, d)


def _fft2_kernel(t_ref, p_ref, q_ref, tc_ref, ts_ref, o_ref, mat):
    cg_n, b_n, _, d = t_ref.shape

    @pl.when(pl.program_id(2) == 0)
    def _():
        mat[...] = (p_ref[...] * tc_ref[...] + q_ref[...] * ts_ref[...]).astype(BF16)

    for cg in range(cg_n):
        y = _dot(mat[...], t_ref[cg].reshape(b_n * PAIR, d)).astype(BF16)
        o_ref[:, cg] = y.reshape(b_n, PAIR, d)


def _fft12_kernel(x_ref, mod_ref, g_ref, m_ref, p_ref, q_ref, tc_ref, ts_ref, o_ref, t1, mat):
    a_n, b_n, _, d = x_ref.shape
    ah_n = a_n // SUBLANES

    @pl.when((pl.program_id(0) == 0) & (pl.program_id(1) == 0))
    def _():
        for h in range(ah_n):
            mat[h] = (p_ref[...] * tc_ref[h] + q_ref[...] * ts_ref[h]).astype(BF16)

    shift, scale = (mod_ref[:, k * d:(k + 1) * d] for k in (0, 1))
    for b in range(b_n):
        hb = _rms_mod(x_ref[:, b].reshape(a_n * SUBLANES, d), g_ref[...], scale, shift).astype(BF16)
        y = _dot(m_ref[...], hb).astype(BF16)
        t1[:, :, b] = y.reshape(ah_n, SUBLANES, PAIR, d)
    for h in range(ah_n):
        for cl in range(SUBLANES):
            y = _dot(mat[h], t1[h, cl].reshape(b_n * PAIR, d)).astype(BF16)
            o_ref[:, h, cl] = y.reshape(b_n, PAIR, d)


def _fft3_kernel(t_ref, x_ref, mod_ref, p_ref, q_ref, tc_ref, ts_ref, cc_ref, sc_ref, w_ref, b_ref,
                 o_ref, yr, yi, *, groups):
    kb_n, ah_n, c_n = t_ref.shape[0], t_ref.shape[1], t_ref.shape[2]
    d = t_ref.shape[-1]
    rows = c_n * SUBLANES
    gate = mod_ref[:, 2 * d:3 * d]
    for j in range(kb_n):
        for h in range(ah_n):
            mat = (p_ref[...] * tc_ref[j, h] + q_ref[...] * ts_ref[j, h]).astype(BF16)
            y = _dot(mat, t_ref[j, h].reshape(c_n * PAIR, d)).astype(BF16)
            r0 = (j * ah_n + h) * rows
            yr[r0:r0 + rows, :] = y[:rows]
            yi[r0:r0 + rows, :] = y[rows:]
    dg = d // groups
    f = jnp.concatenate(
        [_dot(yr[:, g * dg:(g + 1) * dg], cc_ref[...]) + _dot(yi[:, g * dg:(g + 1) * dg], sc_ref[...])
         for g in range(groups)], axis=1)
    m = _dot(f.astype(BF16), w_ref[...]) + b_ref[...]
    for j in range(kb_n):
        for h in range(ah_n):
            r0 = (j * ah_n + h) * rows
            o_ref[:, j, h] = x_ref[:, j, h] + (gate * m[r0:r0 + rows]).reshape(c_n, SUBLANES, d)


def _fourier_layer(x, mod, w, layer, j, *, step_rows=1024):
    bt, s, d = x.shape
    a_n, b_n, c_n = _fft_factors(s)
    assert a_n * b_n * c_n == s and a_n % 8 == 0 and c_n % 8 == 0 and d % FNET_GROUPS == 0
    ah_n = a_n // 8
    dg = d // FNET_GROUPS
    m1, pq2, pq3, tw1, tw2, cc, sc = _fft_tables(s, dg)
    mod_spec = pl.BlockSpec((None, 1, N_ADA * d), lambda *i: (i[0], 0, 0))

    slab_bytes = a_n * b_n * SUBLANES * d * 4
    if 5 * slab_bytes + FFT_TABLE_BYTES <= VMEM_LIMIT_BYTES:
        ch_n = c_n // SUBLANES
        t2 = pl.pallas_call(
            _fft12_kernel,
            out_shape=jax.ShapeDtypeStruct((bt, b_n, ah_n, ch_n, SUBLANES, PAIR, d), BF16),
            grid=(bt, ch_n),
            in_specs=[
                pl.BlockSpec((None, a_n, b_n, None, SUBLANES, d), lambda i, c: (i, 0, 0, c, 0, 0)),
                mod_spec,
                _layer_slice((1, d), layer),
                _resident(m1.shape),
                _resident(pq2[0].shape), _resident(pq2[1].shape),
                _resident(tw1[0].shape), _resident(tw1[1].shape),
            ],
            out_specs=pl.BlockSpec((None, b_n, ah_n, None, SUBLANES, PAIR, d),
                                   lambda i, c: (i, 0, 0, c, 0, 0, 0)),
            scratch_shapes=[pltpu.VMEM((ah_n, SUBLANES, b_n, PAIR, d), BF16),
                            pltpu.VMEM((ah_n, PAIR * b_n, PAIR * b_n), BF16)],
            compiler_params=_params(2),
            name="fourier_seq12",
        )(x.reshape(bt, a_n, b_n, ch_n, SUBLANES, d), mod, w["norm_mix"], m1, *pq2, *tw1)
        t2 = t2.reshape(bt, b_n, ah_n, c_n, PAIR, d)
    else:
        t2 = _fourier_stages_1_2(x, mod, w, layer, (a_n, b_n, c_n), m1, pq2, tw1, mod_spec, step_rows)
    return _fourier_stage_3(x, mod, w, j, (a_n, b_n, c_n), t2, pq3, tw2, cc, sc, mod_spec, step_rows)


def _fourier_stages_1_2(x, mod, w, layer, factors, m1, pq2, tw1, mod_spec, step_rows):
    bt, s, d = x.shape
    a_n, b_n, c_n = factors
    ah_n = a_n // 8

    bg = _largest_divisor(b_n, step_rows // (a_n * c_n))
    t1 = pl.pallas_call(
        _fft1_kernel,
        out_shape=jax.ShapeDtypeStruct((bt, ah_n, c_n, b_n, PAIR, d), BF16),
        grid=(bt, b_n // bg),
        in_specs=[
            pl.BlockSpec((None, a_n, bg, c_n, d), lambda i, q: (i, 0, q, 0, 0)),
            mod_spec,
            _layer_slice((1, d), layer),
            _resident(m1.shape),
        ],
        out_specs=pl.BlockSpec((None, ah_n, c_n, bg, PAIR, d), lambda i, q: (i, 0, 0, q, 0, 0)),
        compiler_params=_params(2),
        name="fourier_seq1",
    )(x.reshape(bt, a_n, b_n, c_n, d), mod, w["norm_mix"], m1)

    cg = _largest_divisor(c_n, step_rows // (b_n * SUBLANES))
    tw1_spec = pl.BlockSpec((None, 1, PAIR * b_n), lambda i, h, c: (h, 0, 0))
    return pl.pallas_call(
        _fft2_kernel,
        out_shape=jax.ShapeDtypeStruct((bt, b_n, ah_n, c_n, PAIR, d), BF16),
        grid=(bt, ah_n, c_n // cg),
        in_specs=[
            pl.BlockSpec((None, None, cg, b_n, PAIR, d), lambda i, h, c: (i, h, c, 0, 0, 0)),
            _resident(pq2[0].shape), _resident(pq2[1].shape), tw1_spec, tw1_spec,
        ],
        out_specs=pl.BlockSpec((None, b_n, None, cg, PAIR, d), lambda i, h, c: (i, 0, h, c, 0, 0)),
        scratch_shapes=[pltpu.VMEM((PAIR * b_n, PAIR * b_n), BF16)],
        compiler_params=_params(3),
        name="fourier_seq2",
    )(t1, *pq2, *tw1)


def _fourier_stage_3(x, mod, w, j, factors, t2, pq3, tw2, cc, sc, mod_spec, step_rows):
    bt, s, d = x.shape
    a_n, b_n, c_n = factors
    ah_n = a_n // 8
    dg = d // FNET_GROUPS

    kb_blk = _largest_divisor(b_n, step_rows // (ah_n * c_n * SUBLANES))
    rows = kb_blk * ah_n * c_n * SUBLANES
    xo_spec = pl.BlockSpec((None, c_n, kb_blk, ah_n, SUBLANES, d), lambda i, k: (i, 0, k, 0, 0, 0))
    tw2_spec = pl.BlockSpec((kb_blk, ah_n, 1, PAIR * c_n), lambda i, k: (k, 0, 0, 0))
    out = pl.pallas_call(
        functools.partial(_fft3_kernel, groups=FNET_GROUPS),
        out_shape=jax.ShapeDtypeStruct((bt, c_n, b_n, ah_n, SUBLANES, d), F32),
        grid=(bt, b_n // kb_blk),
        in_specs=[
            pl.BlockSpec((None, kb_blk, ah_n, c_n, PAIR, d), lambda i, k: (i, k, 0, 0, 0, 0)),
            xo_spec, mod_spec,
            _resident(pq3[0].shape), _resident(pq3[1].shape), tw2_spec, tw2_spec,
            _resident((dg, dg)), _resident((dg, dg)),
            _layer_slice((d, d), j),
            _layer_slice((1, d), j),
        ],
        out_specs=xo_spec,
        scratch_shapes=[pltpu.VMEM((rows, d), BF16), pltpu.VMEM((rows, d), BF16)],
        compiler_params=_params(2),
        name="fourier_seq3",
    )(t2, x.reshape(bt, c_n, b_n, ah_n, SUBLANES, d), mod, *pq3, *tw2, cc, sc, w["b_w_out"], w["b_b_out"])
    return out.reshape(bt, s, d)


def _tile_rows(s, target):
    tm = min(s, target)
    while s % tm:
        tm //= 2
    return tm


def _trunk(x, mods, w, *, mlp_tm=1024, conv_tm=1024, ff_chunk=1024):
    depth = w["norm_mix"].shape[0]
    s = x.shape[1]
    for i in range(depth):
        kind, j = i % N_MIXERS, i // N_MIXERS
        if kind == 0:
            x = _sconv_layer(x, mods[i], w, i, j, tm=_tile_rows(s, conv_tm))
        elif kind == 1:
            x = _fourier_layer(x, mods[i], w, i, j)
        else:
            x = _conformer_layer(x, mods[i], w, i, j, tm=_tile_rows(s, conv_tm))
        x = _mlp_layer(x, mods[i], w, i, final=(i == depth - 1), tm=_tile_rows(s, mlp_tm),
                       ff_chunk=ff_chunk)
    return x


def kernel(x_prompt, x_sample, c_prompt, c_sample, ada_w, ada_b, norm_mix, norm_mlp, a_w_in, a_conv_w, a_w_out, b_w_out, b_b_out, c_w_pw1, c_b_pw1, c_dw_w, c_dw_b, c_ln_g, c_ln_b, c_w_pw2, c_b_pw2, mlp_w_up, mlp_w_down, final_norm):
    d = x_prompt.shape[-1]
    n_p, n_s = c_prompt.shape[0], c_sample.shape[0]
    pad_rows = -(n_p + n_s) % SUBLANES
    c_all = jnp.concatenate([c_prompt, c_sample, jnp.zeros((pad_rows, d), F32)], axis=0)
    mod = _ada_modulation(c_all, ada_w, ada_b)
    mods_p = mod[:, :n_p, None, :]
    mods_s = mod[:, n_p:n_p + n_s, None, :]
    w = dict(
        norm_mix=_rows(norm_mix), norm_mlp=_rows(norm_mlp), final_norm=final_norm.reshape(1, d),
        a_w_in=a_w_in.astype(BF16), a_conv_w=a_conv_w, a_w_out=a_w_out.astype(BF16),
        b_w_out=b_w_out.astype(BF16), b_b_out=_rows(b_b_out),
        c_w_pw1=c_w_pw1.astype(BF16), c_b_pw1=_rows(c_b_pw1), c_dw_w=c_dw_w, c_dw_b=_rows(c_dw_b),
        c_ln_g=_rows(c_ln_g), c_ln_b=_rows(c_ln_b), c_w_pw2=c_w_pw2.astype(BF16), c_b_pw2=_rows(c_b_pw2),
        mlp_f32=(mlp_w_up, mlp_w_down), mlp_bf16=[None] * mlp_w_up.shape[0],
    )
    return _trunk(x_prompt, mods_p, w), _trunk(x_sample, mods_s, w)
```

```python
import functools
import math

import jax
import jax.numpy as jnp
from jax import lax
from jax.experimental import pallas as pl
from jax.experimental.pallas import tpu as pltpu

EPS = 1e-6
N_ADA = 6
N_MIXERS = 3
FNET_GROUPS = 4

SUBLANES = 8
LANES = 128
BF16_ROWS = 16
VMEM_LIMIT_BYTES = 56 * 1024 * 1024

F32 = jnp.float32
BF16 = jnp.bfloat16
U32 = jnp.uint32


def _params(n_axes):
    return pltpu.CompilerParams(
        dimension_semantics=("arbitrary",) * n_axes, vmem_limit_bytes=VMEM_LIMIT_BYTES)


def _resident(shape):
    zeros = (0,) * len(shape)
    return pl.BlockSpec(shape, lambda *_: zeros, pipeline_mode=pl.Buffered(1))


def _layer_slice(shape, layer):
    index = (layer,) + (0,) * len(shape)
    return pl.BlockSpec((None,) + shape, lambda *_: index, pipeline_mode=pl.Buffered(1))


def _rows(p):
    return p.reshape(p.shape[0], 1, p.shape[1])


def _rms_mod(x, gain, scale, shift):
    ms = jnp.mean(x * x, axis=-1, keepdims=True)
    return (x * lax.rsqrt(ms + EPS)) * gain * (1.0 + scale) + shift


def _dot(a, b):
    return jnp.dot(a, b, preferred_element_type=F32)


def _ada_kernel(c_ref, w_ref, b_ref, o_ref):
    c = c_ref[...]
    c_act = (c * jax.nn.sigmoid(c)).astype(BF16)
    o_ref[...] = _dot(c_act, w_ref[...].astype(BF16)) + b_ref[...]


def _ada_modulation(c_all, ada_w, ada_b, col_block=1536):
    depth, d, n = ada_w.shape
    rows = c_all.shape[0]
    assert n % col_block == 0
    return pl.pallas_call(
        _ada_kernel,
        out_shape=jax.ShapeDtypeStruct((depth, rows, n), F32),
        grid=(depth, n // col_block),
        in_specs=[
            pl.BlockSpec((rows, d), lambda l, j: (0, 0)),
            pl.BlockSpec((None, d, col_block), lambda l, j: (l, 0, j)),
            pl.BlockSpec((None, 1, col_block), lambda l, j: (l, 0, j)),
        ],
        out_specs=pl.BlockSpec((None, rows, col_block), lambda l, j: (l, 0, j)),
        compiler_params=_params(2),
        name="ada_modulation",
    )(c_all, ada_w, _rows(ada_b))


def _mlp_kernel(x_ref, mod_ref, g_ref, wu_ref, wd_ref, fn_ref, *rest, ff_chunk, final):
    if len(rest) > 1:
        wu_next_ref, wd_next_ref, o_ref, wu_next_out, wd_next_out = rest
        wu_next_out[...] = wu_next_ref[...].astype(BF16)
        wd_next_out[...] = wd_next_ref[...].astype(BF16)
    else:
        o_ref, = rest
    x = x_ref[...]
    d = x.shape[-1]
    shift, scale, gate = (mod_ref[:, k * d:(k + 1) * d] for k in (3, 4, 5))
    h = _rms_mod(x, g_ref[...], scale, shift).astype(BF16)
    acc = jnp.zeros_like(x)
    for j in range(wu_ref.shape[1] // ff_chunk):
        u = jnp.maximum(_dot(h, wu_ref[:, j * ff_chunk:(j + 1) * ff_chunk]), 0.0)
        acc = acc + _dot((u * u).astype(BF16), wd_ref[j * ff_chunk:(j + 1) * ff_chunk, :])
    y = x + gate * acc
    if final:
        y = y * lax.rsqrt(jnp.mean(y * y, axis=-1, keepdims=True) + EPS) * fn_ref[...]
    o_ref[...] = y


def _mlp_layer(x, mod, w, layer, *, final, tm, ff_chunk):
    bt, s, d = x.shape
    if w["mlp_bf16"][layer] is None:
        w["mlp_bf16"][layer] = tuple(p[layer].astype(BF16) for p in w["mlp_f32"])
    w_up, w_down = w["mlp_bf16"][layer]
    f = w_up.shape[-1]
    ff_chunk = min(ff_chunk, f)
    assert s % tm == 0 and f % ff_chunk == 0
    tiles = s // tm
    steps = bt * tiles
    xf = x.reshape(bt * s, d)
    in_specs = [
        pl.BlockSpec((tm, d), lambda i: (i, 0)),
        pl.BlockSpec((None, 1, N_ADA * d), lambda i: (i // tiles, 0, 0)),
        _layer_slice((1, d), layer),
        _resident((d, f)),
        _resident((f, d)),
        _resident((1, d)),
    ]
    args = [xf, mod, w["norm_mlp"], w_up, w_down, w["final_norm"]]
    out_shape = [jax.ShapeDtypeStruct(xf.shape, F32)]
    out_specs = [pl.BlockSpec((tm, d), lambda i: (i, 0))]
    nxt = layer + 1
    cast_next = (nxt < len(w["mlp_bf16"]) and w["mlp_bf16"][nxt] is None
                 and d % (steps * BF16_ROWS) == 0 and f % (steps * BF16_ROWS) == 0)
    if cast_next:
        in_specs += [pl.BlockSpec((None, d // steps, f), lambda i: (nxt, i, 0)),
                     pl.BlockSpec((None, f // steps, d), lambda i: (nxt, i, 0))]
        args += list(w["mlp_f32"])
        out_shape += [jax.ShapeDtypeStruct((d, f), BF16), jax.ShapeDtypeStruct((f, d), BF16)]
        out_specs += [pl.BlockSpec((d // steps, f), lambda i: (i, 0)),
                      pl.BlockSpec((f // steps, d), lambda i: (i, 0))]
    out = pl.pallas_call(
        functools.partial(_mlp_kernel, ff_chunk=ff_chunk, final=final),
        out_shape=out_shape,
        grid=(steps,),
        in_specs=in_specs,
        out_specs=out_specs,
        compiler_params=_params(1),
        name="mlp_final" if final else "mlp",
    )(*args)
    if cast_next:
        w["mlp_bf16"][nxt] = (out[1], out[2])
    return out[0].reshape(bt, s, d)


CHUNK_LANES = 2 * LANES
HALO = BF16_ROWS


def _halo_specs(tm, d, n_rows):
    per = tm // HALO
    last = n_rows // HALO - 1
    return [
        pl.BlockSpec((HALO, d), lambda i: (jnp.maximum(i * per - 1, 0), 0)),
        pl.BlockSpec((tm, d), lambda i: (i, 0)),
        pl.BlockSpec((HALO, d), lambda i: (jnp.minimum((i + 1) * per, last), 0)),
    ]


def _fill_hext(hext, xp_ref, x_ref, xn_ref, gain, scale, shift, tm):
    hext[0:HALO, :] = _rms_mod(xp_ref[...], gain, scale, shift).astype(BF16)
    hext[HALO:HALO + tm, :] = _rms_mod(x_ref[...], gain, scale, shift).astype(BF16)
    hext[HALO + tm:, :] = _rms_mod(xn_ref[...], gain, scale, shift).astype(BF16)


def _zero_outside_sequence(u, tm, tiles):
    t = pl.program_id(0) % tiles
    before = jnp.where(t == 0, 0.0, u[:HALO])
    after = jnp.where(t == tiles - 1, 0.0, u[HALO + tm:])
    return jnp.concatenate([before, u[HALO:HALO + tm], after], axis=0)


def _sconv_kernel(xp_ref, x_ref, xn_ref, mod_ref, g_ref, win_ref, cw_ref, wout_ref, o_ref, hext,
                  *, tm, tiles):
    d = x_ref.shape[-1]
    n = tm + 2 * HALO
    shift, scale, gate = (mod_ref[:, k * d:(k + 1) * d] for k in (0, 1, 2))
    _fill_hext(hext, xp_ref, x_ref, xn_ref, g_ref[...], scale, shift, tm)
    zs = []
    for c0 in range(0, d, CHUNK_LANES):
        cols = slice(c0, c0 + CHUNK_LANES)
        cgate = _dot(hext[...], win_ref[:, d + c0:d + c0 + CHUNK_LANES])
        value = _dot(hext[...], win_ref[:, 2 * d + c0:2 * d + c0 + CHUNK_LANES])
        u = _zero_outside_sequence(cgate * value, tm, tiles)
        prev = pltpu.roll(u, 1, axis=0)[HALO:HALO + tm]
        nxt = pltpu.roll(u, n - 1, axis=0)[HALO:HALO + tm]
        y = cw_ref[0:1, cols] * prev + cw_ref[1:2, cols] * u[HALO:HALO + tm] + cw_ref[2:3, cols] * nxt
        b = _dot(hext[HALO:HALO + tm, :], win_ref[:, cols])
        zs.append((b * y).astype(BF16))
    m = _dot(jnp.concatenate(zs, axis=1), wout_ref[...])
    o_ref[...] = x_ref[...] + gate * m


def _sconv_layer(x, mod, w, layer, j, *, tm):
    bt, s, d = x.shape
    assert s % tm == 0 and tm % HALO == 0 and w["a_conv_w"].shape[1] == 3 and d % CHUNK_LANES == 0
    tiles = s // tm
    xf = x.reshape(bt * s, d)
    out = pl.pallas_call(
        functools.partial(_sconv_kernel, tm=tm, tiles=tiles),
        out_shape=jax.ShapeDtypeStruct(xf.shape, F32),
        grid=(bt * tiles,),
        in_specs=_halo_specs(tm, d, bt * s) + [
            pl.BlockSpec((None, 1, N_ADA * d), lambda i: (i // tiles, 0, 0)),
            _layer_slice((1, d), layer),
            _layer_slice((d, 3 * d), j),
            _layer_slice((3, d), j),
            _layer_slice((d, d), j),
        ],
        out_specs=pl.BlockSpec((tm, d), lambda i: (i, 0)),
        scratch_shapes=[pltpu.VMEM((tm + 2 * HALO, d), BF16)],
        compiler_params=_params(1),
        name="short_conv",
    )(xf, xf, xf, mod, w["norm_mix"], w["a_w_in"], w["a_conv_w"], w["a_w_out"])
    return out.reshape(bt, s, d)


CONV_ROWS = 4 * BF16_ROWS


def _conformer_kernel(xp_ref, x_ref, xn_ref, mod_ref, g_ref, w1_ref, b1_ref, dw_ref, dwb_ref,
                      lng_ref, lnb_ref, w2_ref, b2_ref, o_ref, hext, even, odd, wtap, conv,
                      *, tm, tiles, width):
    d = x_ref.shape[-1]
    n = tm + 2 * HALO
    pad = width // 2
    lane_tiles = d // LANES

    @pl.when(pl.program_id(0) == 0)
    def _():
        for k in range(width):
            for j in range(lane_tiles):
                tap = dw_ref[k:k + 1, j * LANES:(j + 1) * LANES]
                wtap[k, j] = jnp.broadcast_to(tap, (BF16_ROWS, LANES)).astype(BF16)

    shift, scale, gate = (mod_ref[:, k * d:(k + 1) * d] for k in (0, 1, 2))
    _fill_hext(hext, xp_ref, x_ref, xn_ref, g_ref[...], scale, shift, tm)
    for c0 in range(0, d, CHUNK_LANES):
        value = _dot(hext[...], w1_ref[:, c0:c0 + CHUNK_LANES]) + b1_ref[:, c0:c0 + CHUNK_LANES]
        glu = _dot(hext[...], w1_ref[:, d + c0:d + c0 + CHUNK_LANES]) + b1_ref[:, d + c0:d + c0 + CHUNK_LANES]
        u = _zero_outside_sequence(value * jax.nn.sigmoid(glu), tm, tiles)
        pe = pltpu.bitcast(u.astype(BF16), U32)
        po = pltpu.bitcast(pltpu.roll(u, n - 1, axis=0).astype(BF16), U32)
        for j in range(c0 // LANES, (c0 + CHUNK_LANES) // LANES):
            l0 = j * LANES - c0
            even[j] = pe[:, l0:l0 + LANES]
            odd[j] = po[:, l0:l0 + LANES]

    half = BF16_ROWS // 2

    def chunk(ci, carry):
        word0 = pl.multiple_of(ci * (CONV_ROWS // 2), CONV_ROWS // 2)
        row0 = pl.multiple_of(ci * CONV_ROWS, CONV_ROWS)
        for j in range(lane_tiles):
            acc = [jnp.zeros((BF16_ROWS, LANES), F32) for _ in range(CONV_ROWS // BF16_ROWS)]
            for k in range(width):
                off = HALO - pad + k
                src = odd if off % 2 else even
                tap = wtap[k, j].astype(F32)
                for a in range(len(acc)):
                    words = src[j, pl.ds(word0 + off // 2 + a * half, half), :]
                    acc[a] = acc[a] + tap * pltpu.bitcast(words, BF16).astype(F32)
            bias = dwb_ref[:, j * LANES:(j + 1) * LANES]
            for a in range(len(acc)):
                conv[pl.ds(row0 + a * BF16_ROWS, BF16_ROWS), j * LANES:(j + 1) * LANES] = acc[a] + bias
        return carry

    lax.fori_loop(0, tm // CONV_ROWS, chunk, 0)
    v = conv[...]
    mu = jnp.mean(v, axis=-1, keepdims=True)
    vc = v - mu
    ln = vc * lax.rsqrt(jnp.mean(vc * vc, axis=-1, keepdims=True) + EPS) * lng_ref[...] + lnb_ref[...]
    act = (ln * jax.nn.sigmoid(ln)).astype(BF16)
    m = _dot(act, w2_ref[...]) + b2_ref[...]
    o_ref[...] = x_ref[...] + gate * m


def _conformer_layer(x, mod, w, layer, j, *, tm):
    bt, s, d = x.shape
    width = w["c_dw_w"].shape[1]
    assert s % tm == 0 and tm % CONV_ROWS == 0 and width // 2 < HALO and d % CHUNK_LANES == 0
    tiles = s // tm
    n = tm + 2 * HALO
    xf = x.reshape(bt * s, d)
    out = pl.pallas_call(
        functools.partial(_conformer_kernel, tm=tm, tiles=tiles, width=width),
        out_shape=jax.ShapeDtypeStruct(xf.shape, F32),
        grid=(bt * tiles,),
        in_specs=_halo_specs(tm, d, bt * s) + [
            pl.BlockSpec((None, 1, N_ADA * d), lambda i: (i // tiles, 0, 0)),
            _layer_slice((1, d), layer),
            _layer_slice((d, 2 * d), j),
            _layer_slice((1, 2 * d), j),
            _layer_slice((width, d), j),
            _layer_slice((1, d), j),
            _layer_slice((1, d), j),
            _layer_slice((1, d), j),
            _layer_slice((d, d), j),
            _layer_slice((1, d), j),
        ],
        out_specs=pl.BlockSpec((tm, d), lambda i: (i, 0)),
        scratch_shapes=[
            pltpu.VMEM((n, d), BF16),
            pltpu.VMEM((d // LANES, n // 2, LANES), U32),
            pltpu.VMEM((d // LANES, n // 2, LANES), U32),
            pltpu.VMEM((width, d // LANES, BF16_ROWS, LANES), BF16),
            pltpu.VMEM((tm, d), F32),
        ],
        compiler_params=_params(1),
        name="conformer",
    )(xf, xf, xf, mod, w["norm_mix"], w["c_w_pw1"], w["c_b_pw1"], w["c_dw_w"], w["c_dw_b"],
      w["c_ln_g"], w["c_ln_b"], w["c_w_pw2"], w["c_b_pw2"])
    return out.reshape(bt, s, d)


PAIR = 2 * SUBLANES
FFT_TABLE_BYTES = 8 * 1024 * 1024


def _fft_factors(s):
    known = {4096: (16, 16, 16), 16384: (32, 32, 16)}
    if s in known:
        return known[s]
    best = None
    for a in (8, 16, 32, 64):
        for c in (8, 16, 32, 64):
            if s % (a * c) == 0:
                b = s // (a * c)
                cost = max(a, 32) + 2 * max(b, 16) + 2 * max(c, 16)
                if best is None or cost < best[0]:
                    best = (cost, (a, b, c))
    assert best is not None, f"sequence length {s} needs two factors that are multiples of 8"
    return best[1]


def _largest_divisor(n, limit):
    k = max(1, min(n, limit))
    while n % k:
        k -= 1
    return k


def _cos_sin(phase, period):
    ang = phase.astype(F32) * F32(2.0 * math.pi / period)
    return jnp.cos(ang), jnp.sin(ang)


def _stage_tables(n, part_major_rows):
    size = PAIR * n
    r = lax.broadcasted_iota(jnp.int32, (size, size), 0)
    q = lax.broadcasted_iota(jnp.int32, (size, size), 1)
    m, pi = q // PAIR, (q // SUBLANES) % 2
    if part_major_rows:
        po, k = r // (SUBLANES * n), (r // SUBLANES) % n
    else:
        po, k = (r // SUBLANES) % 2, r // PAIR
    co, si = _cos_sin((k * m) % n, n)
    sign = (pi - po).astype(F32)
    same = (r % SUBLANES) == (q % SUBLANES)
    p = jnp.where(same, jnp.where(po == pi, co, sign * si), 0.0)
    qq = jnp.where(same, jnp.where(po == pi, -si, sign * co), 0.0)
    return p, qq


def _fft_tables(s, dg):
    a_n, b_n, c_n = _fft_factors(s)
    i32 = jnp.int32
    ar = functools.partial(jnp.arange, dtype=i32)

    r = lax.broadcasted_iota(i32, (PAIR * a_n, SUBLANES * a_n), 0)
    q = lax.broadcasted_iota(i32, (PAIR * a_n, SUBLANES * a_n), 1)
    ka = (r // (SUBLANES * PAIR)) * SUBLANES + r % SUBLANES
    co, si = _cos_sin((ka * (q // SUBLANES)) % a_n, a_n)
    m1 = jnp.where(((r // PAIR) % SUBLANES) == (q % SUBLANES),
                   jnp.where((r // SUBLANES) % 2 == 0, co, -si), 0.0).astype(BF16)

    pq2 = _stage_tables(b_n, False)
    pq3 = _stage_tables(c_n, True)

    ka = ar(a_n // 8)[:, None, None, None] * 8 + ar(8)[None, None, None, :]
    both = jnp.zeros((1, 1, 2, 1), i32)
    tw1 = _cos_sin((ar(b_n)[None, :, None, None] * ka + both) % (a_n * b_n), a_n * b_n)
    tw1 = [t.reshape(a_n // 8, 1, PAIR * b_n) for t in tw1]
    kk = ka[None] + a_n * ar(b_n)[:, None, None, None, None]
    tw2 = _cos_sin((ar(c_n)[None, None, :, None, None] * kk + both[None]) % s, s)
    tw2 = [t.reshape(b_n, a_n // 8, 1, PAIR * c_n) for t in tw2]

    co, si = _cos_sin((ar(dg)[:, None] * ar(dg)[None, :]) % dg, dg)
    norm = F32(1.0 / math.sqrt(s * dg))
    return m1, pq2, pq3, tw1, tw2, (co * norm).astype(BF16), (si * norm).astype(BF16)


def _fft1_kernel(x_ref, mod_ref, g_ref, m_ref, o_ref):
    a_n, bg_n, c_n, d = x_ref.shape
    shift, scale = (mod_ref[:, k * d:(k + 1) * d] for k in (0, 1))
    for bg in range(bg_n):
        for ch in range(c_n // SUBLANES):
            rows = slice(ch * SUBLANES, (ch + 1) * SUBLANES)
            xs = x_ref[:, bg, rows, :].reshape(a_n * SUBLANES, d)
            h = _rms_mod(xs, g_ref[...], scale, shift).astype(BF16)
            y = _dot(m_ref[...], h).astype(BF16)
            o_ref[:, rows, bg] = y.reshape(a_n // SUBLANES, SUBLANES, PAIR, d)


def _fft2_kernel(t_ref, p_ref, q_ref, tc_ref, ts_ref, o_ref, mat):
    cg_n, b_n, _, d = t_ref.shape

    @pl.when(pl.program_id(2) == 0)
    def _():
        mat[...] = (p_ref[...] * tc_ref[...] + q_ref[...] * ts_ref[...]).astype(BF16)

    for cg in range(cg_n):
        y = _dot(mat[...], t_ref[cg].reshape(b_n * PAIR, d)).astype(BF16)
        o_ref[:, cg] = y.reshape(b_n, PAIR, d)


def _fft12_kernel(x_ref, mod_ref, g_ref, m_ref, p_ref, q_ref, tc_ref, ts_ref, o_ref, t1, mat):
    a_n, b_n, _, d = x_ref.shape
    ah_n = a_n // SUBLANES

    @pl.when((pl.program_id(0) == 0) & (pl.program_id(1) == 0))
    def _():
        for h in range(ah_n):
            mat[h] = (p_ref[...] * tc_ref[h] + q_ref[...] * ts_ref[h]).astype(BF16)

    shift, scale = (mod_ref[:, k * d:(k + 1) * d] for k in (0, 1))
    for b in range(b_n):
        hb = _rms_mod(x_ref[:, b].reshape(a_n * SUBLANES, d), g_ref[...], scale, shift).astype(BF16)
        y = _dot(m_ref[...], hb).astype(BF16)
        t1[:, :, b] = y.reshape(ah_n, SUBLANES, PAIR, d)
    for h in range(ah_n):
        for cl in range(SUBLANES):
            y = _dot(mat[h], t1[h, cl].reshape(b_n * PAIR, d)).astype(BF16)
            o_ref[:, h, cl] = y.reshape(b_n, PAIR, d)


def _fft3_kernel(t_ref, x_ref, mod_ref, p_ref, q_ref, tc_ref, ts_ref, cc_ref, sc_ref, w_ref, b_ref,
                 o_ref, yr, yi, *, groups):
    kb_n, ah_n, c_n = t_ref.shape[0], t_ref.shape[1], t_ref.shape[2]
    d = t_ref.shape[-1]
    rows = c_n * SUBLANES
    gate = mod_ref[:, 2 * d:3 * d]
    for j in range(kb_n):
        for h in range(ah_n):
            mat = (p_ref[...] * tc_ref[j, h] + q_ref[...] * ts_ref[j, h]).astype(BF16)
            y = _dot(mat, t_ref[j, h].reshape(c_n * PAIR, d)).astype(BF16)
            r0 = (j * ah_n + h) * rows
            yr[r0:r0 + rows, :] = y[:rows]
            yi[r0:r0 + rows, :] = y[rows:]
    dg = d // groups
    f = jnp.concatenate(
        [_dot(yr[:, g * dg:(g + 1) * dg], cc_ref[...]) + _dot(yi[:, g * dg:(g + 1) * dg], sc_ref[...])
         for g in range(groups)], axis=1)
    m = _dot(f.astype(BF16), w_ref[...]) + b_ref[...]
    for j in range(kb_n):
        for h in range(ah_n):
            r0 = (j * ah_n + h) * rows
            o_ref[:, j, h] = x_ref[:, j, h] + (gate * m[r0:r0 + rows]).reshape(c_n, SUBLANES, d)


def _fourier_layer(x, mod, w, layer, j, *, step_rows=1024):
    bt, s, d = x.shape
    a_n, b_n, c_n = _fft_factors(s)
    assert a_n * b_n * c_n == s and a_n % 8 == 0 and c_n % 8 == 0 and d % FNET_GROUPS == 0
    ah_n = a_n // 8
    dg = d // FNET_GROUPS
    m1, pq2, pq3, tw1, tw2, cc, sc = _fft_tables(s, dg)
    mod_spec = pl.BlockSpec((None, 1, N_ADA * d), lambda *i: (i[0], 0, 0))

    slab_bytes = a_n * b_n * SUBLANES * d * 4
    if 5 * slab_bytes + FFT_TABLE_BYTES <= VMEM_LIMIT_BYTES:
        ch_n = c_n // SUBLANES
        t2 = pl.pallas_call(
            _fft12_kernel,
            out_shape=jax.ShapeDtypeStruct((bt, b_n, ah_n, ch_n, SUBLANES, PAIR, d), BF16),
            grid=(bt, ch_n),
            in_specs=[
                pl.BlockSpec((None, a_n, b_n, None, SUBLANES, d), lambda i, c: (i, 0, 0, c, 0, 0)),
                mod_spec,
                _layer_slice((1, d), layer),
                _resident(m1.shape),
                _resident(pq2[0].shape), _resident(pq2[1].shape),
                _resident(tw1[0].shape), _resident(tw1[1].shape),
            ],
            out_specs=pl.BlockSpec((None, b_n, ah_n, None, SUBLANES, PAIR, d),
                                   lambda i, c: (i, 0, 0, c, 0, 0, 0)),
            scratch_shapes=[pltpu.VMEM((ah_n, SUBLANES, b_n, PAIR, d), BF16),
                            pltpu.VMEM((ah_n, PAIR * b_n, PAIR * b_n), BF16)],
            compiler_params=_params(2),
            name="fourier_seq12",
        )(x.reshape(bt, a_n, b_n, ch_n, SUBLANES, d), mod, w["norm_mix"], m1, *pq2, *tw1)
        t2 = t2.reshape(bt, b_n, ah_n, c_n, PAIR, d)
    else:
        t2 = _fourier_stages_1_2(x, mod, w, layer, (a_n, b_n, c_n), m1, pq2, tw1, mod_spec, step_rows)
    return _fourier_stage_3(x, mod, w, j, (a_n, b_n, c_n), t2, pq3, tw2, cc, sc, mod_spec, step_rows)


def _fourier_stages_1_2(x, mod, w, layer, factors, m1, pq2, tw1, mod_spec, step_rows):
    bt, s, d = x.shape
    a_n, b_n, c_n = factors
    ah_n = a_n // 8

    bg = _largest_divisor(b_n, step_rows // (a_n * c_n))
    t1 = pl.pallas_call(
        _fft1_kernel,
        out_shape=jax.ShapeDtypeStruct((bt, ah_n, c_n, b_n, PAIR, d), BF16),
        grid=(bt, b_n // bg),
        in_specs=[
            pl.BlockSpec((None, a_n, bg, c_n, d), lambda i, q: (i, 0, q, 0, 0)),
            mod_spec,
            _layer_slice((1, d), layer),
            _resident(m1.shape),
        ],
        out_specs=pl.BlockSpec((None, ah_n, c_n, bg, PAIR, d), lambda i, q: (i, 0, 0, q, 0, 0)),
        compiler_params=_params(2),
        name="fourier_seq1",
    )(x.reshape(bt, a_n, b_n, c_n, d), mod, w["norm_mix"], m1)

    cg = _largest_divisor(c_n, step_rows // (b_n * SUBLANES))
    tw1_spec = pl.BlockSpec((None, 1, PAIR * b_n), lambda i, h, c: (h, 0, 0))
    return pl.pallas_call(
        _fft2_kernel,
        out_shape=jax.ShapeDtypeStruct((bt, b_n, ah_n, c_n, PAIR, d), BF16),
        grid=(bt, ah_n, c_n // cg),
        in_specs=[
            pl.BlockSpec((None, None, cg, b_n, PAIR, d), lambda i, h, c: (i, h, c, 0, 0, 0)),
            _resident(pq2[0].shape), _resident(pq2[1].shape), tw1_spec, tw1_spec,
        ],
        out_specs=pl.BlockSpec((None, b_n, None, cg, PAIR, d), lambda i, h, c: (i, 0, h, c, 0, 0)),
        scratch_shapes=[pltpu.VMEM((PAIR * b_n, PAIR * b_n), BF16)],
        compiler_params=_params(3),
        name="fourier_seq2",
    )(t1, *pq2, *tw1)


def _fourier_stage_3(x, mod, w, j, factors, t2, pq3, tw2, cc, sc, mod_spec, step_rows):
    bt, s, d = x.shape
    a_n, b_n, c_n = factors
    ah_n = a_n // 8
    dg = d // FNET_GROUPS

    kb_blk = _largest_divisor(b_n, step_rows // (ah_n * c_n * SUBLANES))
    rows = kb_blk * ah_n * c_n * SUBLANES
    xo_spec = pl.BlockSpec((None, c_n, kb_blk, ah_n, SUBLANES, d), lambda i, k: (i, 0, k, 0, 0, 0))
    tw2_spec = pl.BlockSpec((kb_blk, ah_n, 1, PAIR * c_n), lambda i, k: (k, 0, 0, 0))
    out = pl.pallas_call(
        functools.partial(_fft3_kernel, groups=FNET_GROUPS),
        out_shape=jax.ShapeDtypeStruct((bt, c_n, b_n, ah_n, SUBLANES, d), F32),
        grid=(bt, b_n // kb_blk),
        in_specs=[
            pl.BlockSpec((None, kb_blk, ah_n, c_n, PAIR, d), lambda i, k: (i, k, 0, 0, 0, 0)),
            xo_spec, mod_spec,
            _resident(pq3[0].shape), _resident(pq3[1].shape), tw2_spec, tw2_spec,
            _resident((dg, dg)), _resident((dg, dg)),
            _layer_slice((d, d), j),
            _layer_slice((1, d), j),
        ],
        out_specs=xo_spec,
        scratch_shapes=[pltpu.VMEM((rows, d), BF16), pltpu.VMEM((rows, d), BF16)],
        compiler_params=_params(2),
        name="fourier_seq3",
    )(t2, x.reshape(bt, c_n, b_n, ah_n, SUBLANES, d), mod, *pq3, *tw2, cc, sc, w["b_w_out"], w["b_b_out"])
    return out.reshape(bt, s, d)


def _tile_rows(s, target):
    tm = min(s, target)
    while s % tm:
        tm //= 2
    return tm


def _trunk(x, mods, w, *, mlp_tm=1024, conv_tm=1024, ff_chunk=1024):
    depth = w["norm_mix"].shape[0]
    s = x.shape[1]
    for i in range(depth):
        kind, j = i % N_MIXERS, i // N_MIXERS
        if kind == 0:
            x = _sconv_layer(x, mods[i], w, i, j, tm=_tile_rows(s, conv_tm))
        elif kind == 1:
            x = _fourier_layer(x, mods[i], w, i, j)
        else:
            x = _conformer_layer(x, mods[i], w, i, j, tm=_tile_rows(s, conv_tm))
        x = _mlp_layer(x, mods[i], w, i, final=(i == depth - 1), tm=_tile_rows(s, mlp_tm),
                       ff_chunk=ff_chunk)
    return x


def kernel(x_prompt, x_sample, c_prompt, c_sample, ada_w, ada_b, norm_mix, norm_mlp, a_w_in, a_conv_w, a_w_out, b_w_out, b_b_out, c_w_pw1, c_b_pw1, c_dw_w, c_dw_b, c_ln_g, c_ln_b, c_w_pw2, c_b_pw2, mlp_w_up, mlp_w_down, final_norm):
    d = x_prompt.shape[-1]
    n_p, n_s = c_prompt.shape[0], c_sample.shape[0]
    pad_rows = -(n_p + n_s) % SUBLANES
    c_all = jnp.concatenate([c_prompt, c_sample, jnp.zeros((pad_rows, d), F32)], axis=0)
    mod = _ada_modulation(c_all, ada_w, ada_b)
    mods_p = mod[:, :n_p, None, :]
    mods_s = mod[:, n_p:n_p + n_s, None, :]
    w = dict(
        norm_mix=_rows(norm_mix), norm_mlp=_rows(norm_mlp), final_norm=final_norm.reshape(1, d),
        a_w_in=a_w_in.astype(BF16), a_conv_w=a_conv_w, a_w_out=a_w_out.astype(BF16),
        b_w_out=b_w_out.astype(BF16), b_b_out=_rows(b_b_out),
        c_w_pw1=c_w_pw1.astype(BF16), c_b_pw1=_rows(c_b_pw1), c_dw_w=c_dw_w, c_dw_b=_rows(c_dw_b),
        c_ln_g=_rows(c_ln_g), c_ln_b=_rows(c_ln_b), c_w_pw2=c_w_pw2.astype(BF16), c_b_pw2=_rows(c_b_pw2),
        mlp_f32=(mlp_w_up, mlp_w_down), mlp_bf16=[None] * mlp_w_up.shape[0],
    )
    return _trunk(x_prompt, mods_p, w), _trunk(x_sample, mods_s, w)
```

```python
import functools
import math

import jax
import jax.numpy as jnp
from jax import lax
from jax.experimental import pallas as pl
from jax.experimental.pallas import tpu as pltpu

EPS = 1e-6
N_ADA = 6
N_MIXERS = 3
FNET_GROUPS = 4

SUBLANES = 8
LANES = 128
BF16_ROWS = 16
VMEM_LIMIT_BYTES = 56 * 1024 * 1024

F32 = jnp.float32
BF16 = jnp.bfloat16
U32 = jnp.uint32


def _params(n_axes):
    return pltpu.CompilerParams(
        dimension_semantics=("arbitrary",) * n_axes, vmem_limit_bytes=VMEM_LIMIT_BYTES)


def _resident(shape):
    zeros = (0,) * len(shape)
    return pl.BlockSpec(shape, lambda *_: zeros, pipeline_mode=pl.Buffered(1))


def _layer_slice(shape, layer):
    index = (layer,) + (0,) * len(shape)
    return pl.BlockSpec((None,) + shape, lambda *_: index, pipeline_mode=pl.Buffered(1))


def _rows(p):
    return p.reshape(p.shape[0], 1, p.shape[1])


def _rms_mod(x, gain, scale, shift):
    ms = jnp.mean(x * x, axis=-1, keepdims=True)
    return (x * lax.rsqrt(ms + EPS)) * gain * (1.0 + scale) + shift


def _dot(a, b):
    return jnp.dot(a, b, preferred_element_type=F32)


def _ada_kernel(c_ref, w_ref, b_ref, o_ref):
    c = c_ref[...]
    c_act = (c * jax.nn.sigmoid(c)).astype(BF16)
    o_ref[...] = _dot(c_act, w_ref[...].astype(BF16)) + b_ref[...]


def _ada_modulation(c_all, ada_w, ada_b, col_block=1536):
    depth, d, n = ada_w.shape
    rows = c_all.shape[0]
    assert n % col_block == 0
    return pl.pallas_call(
        _ada_kernel,
        out_shape=jax.ShapeDtypeStruct((depth, rows, n), F32),
        grid=(depth, n // col_block),
        in_specs=[
            pl.BlockSpec((rows, d), lambda l, j: (0, 0)),
            pl.BlockSpec((None, d, col_block), lambda l, j: (l, 0, j)),
            pl.BlockSpec((None, 1, col_block), lambda l, j: (l, 0, j)),
        ],
        out_specs=pl.BlockSpec((None, rows, col_block), lambda l, j: (l, 0, j)),
        compiler_params=_params(2),
        name="ada_modulation",
    )(c_all, ada_w, _rows(ada_b))


def _mlp_kernel(x_ref, mod_ref, g_ref, wu_ref, wd_ref, fn_ref, *rest, ff_chunk, final):
    if len(rest) > 1:
        wu_next_ref, wd_next_ref, o_ref, wu_next_out, wd_next_out = rest
        wu_next_out[...] = wu_next_ref[...].astype(BF16)
        wd_next_out[...] = wd_next_ref[...].astype(BF16)
    else:
        o_ref, = rest
    x = x_ref[...]
    d = x.shape[-1]
    shift, scale, gate = (mod_ref[:, k * d:(k + 1) * d] for k in (3, 4, 5))
    h = _rms_mod(x, g_ref[...], scale, shift).astype(BF16)
    acc = jnp.zeros_like(x)
    for j in range(wu_ref.shape[1] // ff_chunk):
        u = jnp.maximum(_dot(h, wu_ref[:, j * ff_chunk:(j + 1) * ff_chunk]), 0.0)
        acc = acc + _dot((u * u).astype(BF16), wd_ref[j * ff_chunk:(j + 1) * ff_chunk, :])
    y = x + gate * acc
    if final:
        y = y * lax.rsqrt(jnp.mean(y * y, axis=-1, keepdims=True) + EPS) * fn_ref[...]
    o_ref[...] = y


def _mlp_layer(x, mod, w, layer, *, final, tm, ff_chunk):
    bt, s, d = x.shape
    if w["mlp_bf16"][layer] is None:
        w["mlp_bf16"][layer] = tuple(p[layer].astype(BF16) for p in w["mlp_f32"])
    w_up, w_down = w["mlp_bf16"][layer]
    f = w_up.shape[-1]
    ff_chunk = min(ff_chunk, f)
    assert s % tm == 0 and f % ff_chunk == 0
    tiles = s // tm
    steps = bt * tiles
    xf = x.reshape(bt * s, d)
    in_specs = [
        pl.BlockSpec((tm, d), lambda i: (i, 0)),
        pl.BlockSpec((None, 1, N_ADA * d), lambda i: (i // tiles, 0, 0)),
        _layer_slice((1, d), layer),
        _resident((d, f)),
        _resident((f, d)),
        _resident((1, d)),
    ]
    args = [xf, mod, w["norm_mlp"], w_up, w_down, w["final_norm"]]
    out_shape = [jax.ShapeDtypeStruct(xf.shape, F32)]
    out_specs = [pl.BlockSpec((tm, d), lambda i: (i, 0))]
    nxt = layer + 1
    cast_next = (nxt < len(w["mlp_bf16"]) and w["mlp_bf16"][nxt] is None
                 and d % (steps * BF16_ROWS) == 0 and f % (steps * BF16_ROWS) == 0)
    if cast_next:
        in_specs += [pl.BlockSpec((None, d // steps, f), lambda i: (nxt, i, 0)),
                     pl.BlockSpec((None, f // steps, d), lambda i: (nxt, i, 0))]
        args += list(w["mlp_f32"])
        out_shape += [jax.ShapeDtypeStruct((d, f), BF16), jax.ShapeDtypeStruct((f, d), BF16)]
        out_specs += [pl.BlockSpec((d // steps, f), lambda i: (i, 0)),
                      pl.BlockSpec((f // steps, d), lambda i: (i, 0))]
    out = pl.pallas_call(
        functools.partial(_mlp_kernel, ff_chunk=ff_chunk, final=final),
        out_shape=out_shape,
        grid=(steps,),
        in_specs=in_specs,
        out_specs=out_specs,
        compiler_params=_params(1),
        name="mlp_final" if final else "mlp",
    )(*args)
    if cast_next:
        w["mlp_bf16"][nxt] = (out[1], out[2])
    return out[0].reshape(bt, s, d)


CHUNK_LANES = 2 * LANES
HALO = BF16_ROWS


def _halo_specs(tm, d, n_rows):
    per = tm // HALO
    last = n_rows // HALO - 1
    return [
        pl.BlockSpec((HALO, d), lambda i: (jnp.maximum(i * per - 1, 0), 0)),
        pl.BlockSpec((tm, d), lambda i: (i, 0)),
        pl.BlockSpec((HALO, d), lambda i: (jnp.minimum((i + 1) * per, last), 0)),
    ]


def _fill_hext(hext, xp_ref, x_ref, xn_ref, gain, scale, shift, tm):
    hext[0:HALO, :] = _rms_mod(xp_ref[...], gain, scale, shift).astype(BF16)
    hext[HALO:HALO + tm, :] = _rms_mod(x_ref[...], gain, scale, shift).astype(BF16)
    hext[HALO + tm:, :] = _rms_mod(xn_ref[...], gain, scale, shift).astype(BF16)


def _zero_outside_sequence(u, tm, tiles):
    t = pl.program_id(0) % tiles
    before = jnp.where(t == 0, 0.0, u[:HALO])
    after = jnp.where(t == tiles - 1, 0.0, u[HALO + tm:])
    return jnp.concatenate([before, u[HALO:HALO + tm], after], axis=0)


def _sconv_kernel(xp_ref, x_ref, xn_ref, mod_ref, g_ref, win_ref, cw_ref, wout_ref, o_ref, hext,
                  *, tm, tiles):
    d = x_ref.shape[-1]
    n = tm + 2 * HALO
    shift, scale, gate = (mod_ref[:, k * d:(k + 1) * d] for k in (0, 1, 2))
    _fill_hext(hext, xp_ref, x_ref, xn_ref, g_ref[...], scale, shift, tm)
    zs = []
    for c0 in range(0, d, CHUNK_LANES):
        cols = slice(c0, c0 + CHUNK_LANES)
        cgate = _dot(hext[...], win_ref[:, d + c0:d + c0 + CHUNK_LANES])
        value = _dot(hext[...], win_ref[:, 2 * d + c0:2 * d + c0 + CHUNK_LANES])
        u = _zero_outside_sequence(cgate * value, tm, tiles)
        prev = pltpu.roll(u, 1, axis=0)[HALO:HALO + tm]
        nxt = pltpu.roll(u, n - 1, axis=0)[HALO:HALO + tm]
        y = cw_ref[0:1, cols] * prev + cw_ref[1:2, cols] * u[HALO:HALO + tm] + cw_ref[2:3, cols] * nxt
        b = _dot(hext[HALO:HALO + tm, :], win_ref[:, cols])
        zs.append((b * y).astype(BF16))
    m = _dot(jnp.concatenate(zs, axis=1), wout_ref[...])
    o_ref[...] = x_ref[...] + gate * m


def _sconv_layer(x, mod, w, layer, j, *, tm):
    bt, s, d = x.shape
    assert s % tm == 0 and tm % HALO == 0 and w["a_conv_w"].shape[1] == 3 and d % CHUNK_LANES == 0
    tiles = s // tm
    xf = x.reshape(bt * s, d)
    out = pl.pallas_call(
        functools.partial(_sconv_kernel, tm=tm, tiles=tiles),
        out_shape=jax.ShapeDtypeStruct(xf.shape, F32),
        grid=(bt * tiles,),
        in_specs=_halo_specs(tm, d, bt * s) + [
            pl.BlockSpec((None, 1, N_ADA * d), lambda i: (i // tiles, 0, 0)),
            _layer_slice((1, d), layer),
            _layer_slice((d, 3 * d), j),
            _layer_slice((3, d), j),
            _layer_slice((d, d), j),
        ],
        out_specs=pl.BlockSpec((tm, d), lambda i: (i, 0)),
        scratch_shapes=[pltpu.VMEM((tm + 2 * HALO, d), BF16)],
        compiler_params=_params(1),
        name="short_conv",
    )(xf, xf, xf, mod, w["norm_mix"], w["a_w_in"], w["a_conv_w"], w["a_w_out"])
    return out.reshape(bt, s, d)


CONV_ROWS = 4 * BF16_ROWS


def _conformer_kernel(xp_ref, x_ref, xn_ref, mod_ref, g_ref, w1_ref, b1_ref, dw_ref, dwb_ref,
                      lng_ref, lnb_ref, w2_ref, b2_ref, o_ref, hext, even, odd, wtap, conv,
                      *, tm, tiles, width):
    d = x_ref.shape[-1]
    n = tm + 2 * HALO
    pad = width // 2
    lane_tiles = d // LANES

    @pl.when(pl.program_id(0) == 0)
    def _():
        for k in range(width):
            for j in range(lane_tiles):
                tap = dw_ref[k:k + 1, j * LANES:(j + 1) * LANES]
                wtap[k, j] = jnp.broadcast_to(tap, (BF16_ROWS, LANES)).astype(BF16)

    shift, scale, gate = (mod_ref[:, k * d:(k + 1) * d] for k in (0, 1, 2))
    _fill_hext(hext, xp_ref, x_ref, xn_ref, g_ref[...], scale, shift, tm)
    for c0 in range(0, d, CHUNK_LANES):
        value = _dot(hext[...], w1_ref[:, c0:c0 + CHUNK_LANES]) + b1_ref[:, c0:c0 + CHUNK_LANES]
        glu = _dot(hext[...], w1_ref[:, d + c0:d + c0 + CHUNK_LANES]) + b1_ref[:, d + c0:d + c0 + CHUNK_LANES]
        u = _zero_outside_sequence(value * jax.nn.sigmoid(glu), tm, tiles)
        pe = pltpu.bitcast(u.astype(BF16), U32)
        po = pltpu.bitcast(pltpu.roll(u, n - 1, axis=0).astype(BF16), U32)
        for j in range(c0 // LANES, (c0 + CHUNK_LANES) // LANES):
            l0 = j * LANES - c0
            even[j] = pe[:, l0:l0 + LANES]
            odd[j] = po[:, l0:l0 + LANES]

    half = BF16_ROWS // 2

    def chunk(ci, carry):
        word0 = pl.multiple_of(ci * (CONV_ROWS // 2), CONV_ROWS // 2)
        row0 = pl.multiple_of(ci * CONV_ROWS, CONV_ROWS)
        for j in range(lane_tiles):
            acc = [jnp.zeros((BF16_ROWS, LANES), F32) for _ in range(CONV_ROWS // BF16_ROWS)]
            for k in range(width):
                off = HALO - pad + k
                src = odd if off % 2 else even
                tap = wtap[k, j].astype(F32)
                for a in range(len(acc)):
                    words = src[j, pl.ds(word0 + off // 2 + a * half, half), :]
                    acc[a] = acc[a] + tap * pltpu.bitcast(words, BF16).astype(F32)
            bias = dwb_ref[:, j * LANES:(j + 1) * LANES]
            for a in range(len(acc)):
                conv[pl.ds(row0 + a * BF16_ROWS, BF16_ROWS), j * LANES:(j + 1) * LANES] = acc[a] + bias
        return carry

    lax.fori_loop(0, tm // CONV_ROWS, chunk, 0)
    for r0 in range(0, tm, tm // 2):
        rows = slice(r0, r0 + tm // 2)
        v = conv[rows, :]
        mu = jnp.mean(v, axis=-1, keepdims=True)
        vc = v - mu
        ln = vc * lax.rsqrt(jnp.mean(vc * vc, axis=-1, keepdims=True) + EPS) * lng_ref[...] + lnb_ref[...]
        act = (ln * jax.nn.sigmoid(ln)).astype(BF16)
        m = _dot(act, w2_ref[...]) + b2_ref[...]
        o_ref[rows, :] = x_ref[rows, :] + gate * m


def _conformer_layer(x, mod, w, layer, j, *, tm):
    bt, s, d = x.shape
    width = w["c_dw_w"].shape[1]
    assert s % tm == 0 and tm % CONV_ROWS == 0 and width // 2 < HALO and d % CHUNK_LANES == 0
    tiles = s // tm
    n = tm + 2 * HALO
    xf = x.reshape(bt * s, d)
    out = pl.pallas_call(
        functools.partial(_conformer_kernel, tm=tm, tiles=tiles, width=width),
        out_shape=jax.ShapeDtypeStruct(xf.shape, F32),
        grid=(bt * tiles,),
        in_specs=_halo_specs(tm, d, bt * s) + [
            pl.BlockSpec((None, 1, N_ADA * d), lambda i: (i // tiles, 0, 0)),
            _layer_slice((1, d), layer),
            _layer_slice((d, 2 * d), j),
            _layer_slice((1, 2 * d), j),
            _layer_slice((width, d), j),
            _layer_slice((1, d), j),
            _layer_slice((1, d), j),
            _layer_slice((1, d), j),
            _layer_slice((d, d), j),
            _layer_slice((1, d), j),
        ],
        out_specs=pl.BlockSpec((tm, d), lambda i: (i, 0)),
        scratch_shapes=[
            pltpu.VMEM((n, d), BF16),
            pltpu.VMEM((d // LANES, n // 2, LANES), U32),
            pltpu.VMEM((d // LANES, n // 2, LANES), U32),
            pltpu.VMEM((width, d // LANES, BF16_ROWS, LANES), BF16),
            pltpu.VMEM((tm, d), F32),
        ],
        compiler_params=_params(1),
        name="conformer",
    )(xf, xf, xf, mod, w["norm_mix"], w["c_w_pw1"], w["c_b_pw1"], w["c_dw_w"], w["c_dw_b"],
      w["c_ln_g"], w["c_ln_b"], w["c_w_pw2"], w["c_b_pw2"])
    return out.reshape(bt, s, d)


PAIR = 2 * SUBLANES
FFT_TABLE_BYTES = 8 * 1024 * 1024


def _fft_factors(s):
    known = {4096: (16, 16, 16), 16384: (32, 32, 16)}
    if s in known:
        return known[s]
    best = None
    for a in (8, 16, 32, 64):
        for c in (8, 16, 32, 64):
            if s % (a * c) == 0:
                b = s // (a * c)
                cost = max(a, 32) + 2 * max(b, 16) + 2 * max(c, 16)
                if best is None or cost < best[0]:
                    best = (cost, (a, b, c))
    assert best is not None, f"sequence length {s} needs two factors that are multiples of 8"
    return best[1]


def _largest_divisor(n, limit):
    k = max(1, min(n, limit))
    while n % k:
        k -= 1
    return k


def _cos_sin(phase, period):
    ang = phase.astype(F32) * F32(2.0 * math.pi / period)
    return jnp.cos(ang), jnp.sin(ang)


def _stage_tables(n, part_major_rows):
    size = PAIR * n
    r = lax.broadcasted_iota(jnp.int32, (size, size), 0)
    q = lax.broadcasted_iota(jnp.int32, (size, size), 1)
    m, pi = q // PAIR, (q // SUBLANES) % 2
    if part_major_rows:
        po, k = r // (SUBLANES * n), (r // SUBLANES) % n
    else:
        po, k = (r // SUBLANES) % 2, r // PAIR
    co, si = _cos_sin((k * m) % n, n)
    sign = (pi - po).astype(F32)
    same = (r % SUBLANES) == (q % SUBLANES)
    p = jnp.where(same, jnp.where(po == pi, co, sign * si), 0.0)
    qq = jnp.where(same, jnp.where(po == pi, -si, sign * co), 0.0)
    return p, qq


def _fft_tables(s, dg):
    a_n, b_n, c_n = _fft_factors(s)
    i32 = jnp.int32
    ar = functools.partial(jnp.arange, dtype=i32)

    r = lax.broadcasted_iota(i32, (PAIR * a_n, SUBLANES * a_n), 0)
    q = lax.broadcasted_iota(i32, (PAIR * a_n, SUBLANES * a_n), 1)
    ka = (r // (SUBLANES * PAIR)) * SUBLANES + r % SUBLANES
    co, si = _cos_sin((ka * (q // SUBLANES)) % a_n, a_n)
    m1 = jnp.where(((r // PAIR) % SUBLANES) == (q % SUBLANES),
                   jnp.where((r // SUBLANES) % 2 == 0, co, -si), 0.0).astype(BF16)

    pq2 = _stage_tables(b_n, False)
    pq3 = _stage_tables(c_n, True)

    ka = ar(a_n // 8)[:, None, None, None] * 8 + ar(8)[None, None, None, :]
    both = jnp.zeros((1, 1, 2, 1), i32)
    tw1 = _cos_sin((ar(b_n)[None, :, None, None] * ka + both) % (a_n * b_n), a_n * b_n)
    tw1 = [t.reshape(a_n // 8, 1, PAIR * b_n) for t in tw1]
    kk = ka[None] + a_n * ar(b_n)[:, None, None, None, None]
    tw2 = _cos_sin((ar(c_n)[None, None, :, None, None] * kk + both[None]) % s, s)
    tw2 = [t.reshape(b_n, a_n // 8, 1, PAIR * c_n) for t in tw2]

    co, si = _cos_sin((ar(dg)[:, None] * ar(dg)[None, :]) % dg, dg)
    norm = F32(1.0 / math.sqrt(s * dg))
    return m1, pq2, pq3, tw1, tw2, (co * norm).astype(BF16), (si * norm).astype(BF16)


def _fft1_kernel(x_ref, mod_ref, g_ref, m_ref, o_ref):
    a_n, bg_n, c_n, d = x_ref.shape
    shift, scale = (mod_ref[:, k * d:(k + 1) * d] for k in (0, 1))
    for bg in range(bg_n):
        for ch in range(c_n // SUBLANES):
            rows = slice(ch * SUBLANES, (ch + 1) * SUBLANES)
            xs = x_ref[:, bg, rows, :].reshape(a_n * SUBLANES, d)
            h = _rms_mod(xs, g_ref[...], scale, shift).astype(BF16)
            y = _dot(m_ref[...], h).astype(BF16)
            o_ref[:, rows, bg] = y.reshape(a_n // SUBLANES, SUBLANES, PAIR, d)


def _fft2_kernel(t_ref, p_ref, q_ref, tc_ref, ts_ref, o_ref, mat):
    cg_n, b_n, _, d = t_ref.shape

    @pl.when(pl.program_id(2) == 0)
    def _():
        mat[...] = (p_ref[...] * tc_ref[...] + q_ref[...] * ts_ref[...]).astype(BF16)

    for cg in range(cg_n):
        y = _dot(mat[...], t_ref[cg].reshape(b_n * PAIR, d)).astype(BF16)
        o_ref[:, cg] = y.reshape(b_n, PAIR, d)


def _fft12_kernel(x_ref, mod_ref, g_ref, m_ref, p_ref, q_ref, tc_ref, ts_ref, o_ref, t1, mat):
    a_n, b_n, _, d = x_ref.shape
    ah_n = a_n // SUBLANES

    @pl.when((pl.program_id(0) == 0) & (pl.program_id(1) == 0))
    def _():
        for h in range(ah_n):
            mat[h] = (p_ref[...] * tc_ref[h] + q_ref[...] * ts_ref[h]).astype(BF16)

    shift, scale = (mod_ref[:, k * d:(k + 1) * d] for k in (0, 1))
    for b in range(b_n):
        hb = _rms_mod(x_ref[:, b].reshape(a_n * SUBLANES, d), g_ref[...], scale, shift).astype(BF16)
        y = _dot(m_ref[...], hb).astype(BF16)
        t1[:, :, b] = y.reshape(ah_n, SUBLANES, PAIR, d)
    for h in range(ah_n):
        for cl in range(SUBLANES):
            y = _dot(mat[h], t1[h, cl].reshape(b_n * PAIR, d)).astype(BF16)
            o_ref[:, h, cl] = y.reshape(b_n, PAIR, d)


def _fft3_kernel(t_ref, x_ref, mod_ref, p_ref, q_ref, tc_ref, ts_ref, cc_ref, sc_ref, w_ref, b_ref,
                 o_ref, yr, yi, *, groups):
    kb_n, ah_n, c_n = t_ref.shape[0], t_ref.shape[1], t_ref.shape[2]
    d = t_ref.shape[-1]
    rows = c_n * SUBLANES
    gate = mod_ref[:, 2 * d:3 * d]
    for j in range(kb_n):
        for h in range(ah_n):
            mat = (p_ref[...] * tc_ref[j, h] + q_ref[...] * ts_ref[j, h]).astype(BF16)
            y = _dot(mat, t_ref[j, h].reshape(c_n * PAIR, d)).astype(BF16)
            r0 = (j * ah_n + h) * rows
            yr[r0:r0 + rows, :] = y[:rows]
            yi[r0:r0 + rows, :] = y[rows:]
    dg = d // groups
    f = jnp.concatenate(
        [_dot(yr[:, g * dg:(g + 1) * dg], cc_ref[...]) + _dot(yi[:, g * dg:(g + 1) * dg], sc_ref[...])
         for g in range(groups)], axis=1)
    m = _dot(f.astype(BF16), w_ref[...]) + b_ref[...]
    for j in range(kb_n):
        for h in range(ah_n):
            r0 = (j * ah_n + h) * rows
            o_ref[:, j, h] = x_ref[:, j, h] + (gate * m[r0:r0 + rows]).reshape(c_n, SUBLANES, d)


def _fourier_layer(x, mod, w, layer, j, *, step_rows=1024):
    bt, s, d = x.shape
    a_n, b_n, c_n = _fft_factors(s)
    assert a_n * b_n * c_n == s and a_n % 8 == 0 and c_n % 8 == 0 and d % FNET_GROUPS == 0
    ah_n = a_n // 8
    dg = d // FNET_GROUPS
    m1, pq2, pq3, tw1, tw2, cc, sc = _fft_tables(s, dg)
    mod_spec = pl.BlockSpec((None, 1, N_ADA * d), lambda *i: (i[0], 0, 0))

    slab_bytes = a_n * b_n * SUBLANES * d * 4
    if 5 * slab_bytes + FFT_TABLE_BYTES <= VMEM_LIMIT_BYTES:
        ch_n = c_n // SUBLANES
        t2 = pl.pallas_call(
            _fft12_kernel,
            out_shape=jax.ShapeDtypeStruct((bt, b_n, ah_n, ch_n, SUBLANES, PAIR, d), BF16),
            grid=(bt, ch_n),
            in_specs=[
                pl.BlockSpec((None, a_n, b_n, None, SUBLANES, d), lambda i, c: (i, 0, 0, c, 0, 0)),
                mod_spec,
                _layer_slice((1, d), layer),
                _resident(m1.shape),
                _resident(pq2[0].shape), _resident(pq2[1].shape),
                _resident(tw1[0].shape), _resident(tw1[1].shape),
            ],
            out_specs=pl.BlockSpec((None, b_n, ah_n, None, SUBLANES, PAIR, d),
                                   lambda i, c: (i, 0, 0, c, 0, 0, 0)),
            scratch_shapes=[pltpu.VMEM((ah_n, SUBLANES, b_n, PAIR, d), BF16),
                            pltpu.VMEM((ah_n, PAIR * b_n, PAIR * b_n), BF16)],
            compiler_params=_params(2),
            name="fourier_seq12",
        )(x.reshape(bt, a_n, b_n, ch_n, SUBLANES, d), mod, w["norm_mix"], m1, *pq2, *tw1)
        t2 = t2.reshape(bt, b_n, ah_n, c_n, PAIR, d)
    else:
        t2 = _fourier_stages_1_2(x, mod, w, layer, (a_n, b_n, c_n), m1, pq2, tw1, mod_spec, step_rows)
    return _fourier_stage_3(x, mod, w, j, (a_n, b_n, c_n), t2, pq3, tw2, cc, sc, mod_spec, step_rows)


def _fourier_stages_1_2(x, mod, w, layer, factors, m1, pq2, tw1, mod_spec, step_rows):
    bt, s, d = x.shape
    a_n, b_n, c_n = factors
    ah_n = a_n // 8

    bg = _largest_divisor(b_n, step_rows // (a_n * c_n))
    t1 = pl.pallas_call(
        _fft1_kernel,
        out_shape=jax.ShapeDtypeStruct((bt, ah_n, c_n, b_n, PAIR, d), BF16),
        grid=(bt, b_n // bg),
        in_specs=[
            pl.BlockSpec((None, a_n, bg, c_n, d), lambda i, q: (i, 0, q, 0, 0)),
            mod_spec,
            _layer_slice((1, d), layer),
            _resident(m1.shape),
        ],
        out_specs=pl.BlockSpec((None, ah_n, c_n, bg, PAIR, d), lambda i, q: (i, 0, 0, q, 0, 0)),
        compiler_params=_params(2),
        name="fourier_seq1",
    )(x.reshape(bt, a_n, b_n, c_n, d), mod, w["norm_mix"], m1)

    cg = _largest_divisor(c_n, step_rows // (b_n * SUBLANES))
    tw1_spec = pl.BlockSpec((None, 1, PAIR * b_n), lambda i, h, c: (h, 0, 0))
    return pl.pallas_call(
        _fft2_kernel,
        out_shape=jax.ShapeDtypeStruct((bt, b_n, ah_n, c_n, PAIR, d), BF16),
        grid=(bt, ah_n, c_n // cg),
        in_specs=[
            pl.BlockSpec((None, None, cg, b_n, PAIR, d), lambda i, h, c: (i, h, c, 0, 0, 0)),
            _resident(pq2[0].shape), _resident(pq2[1].shape), tw1_spec, tw1_spec,
        ],
        out_specs=pl.BlockSpec((None, b_n, None, cg, PAIR, d), lambda i, h, c: (i, 0, h, c, 0, 0)),
        scratch_shapes=[pltpu.VMEM((PAIR * b_n, PAIR * b_n), BF16)],
        compiler_params=_params(3),
        name="fourier_seq2",
    )(t1, *pq2, *tw1)


def _fourier_stage_3(x, mod, w, j, factors, t2, pq3, tw2, cc, sc, mod_spec, step_rows):
    bt, s, d = x.shape
    a_n, b_n, c_n = factors
    ah_n = a_n // 8
    dg = d // FNET_GROUPS

    kb_blk = _largest_divisor(b_n, step_rows // (ah_n * c_n * SUBLANES))
    rows = kb_blk * ah_n * c_n * SUBLANES
    xo_spec = pl.BlockSpec((None, c_n, kb_blk, ah_n, SUBLANES, d), lambda i, k: (i, 0, k, 0, 0, 0))
    tw2_spec = pl.BlockSpec((kb_blk, ah_n, 1, PAIR * c_n), lambda i, k: (k, 0, 0, 0))
    out = pl.pallas_call(
        functools.partial(_fft3_kernel, groups=FNET_GROUPS),
        out_shape=jax.ShapeDtypeStruct((bt, c_n, b_n, ah_n, SUBLANES, d), F32),
        grid=(bt, b_n // kb_blk),
        in_specs=[
            pl.BlockSpec((None, kb_blk, ah_n, c_n, PAIR, d), lambda i, k: (i, k, 0, 0, 0, 0)),
            xo_spec, mod_spec,
            _resident(pq3[0].shape), _resident(pq3[1].shape), tw2_spec, tw2_spec,
            _resident((dg, dg)), _resident((dg, dg)),
            _layer_slice((d, d), j),
            _layer_slice((1, d), j),
        ],
        out_specs=xo_spec,
        scratch_shapes=[pltpu.VMEM((rows, d), BF16), pltpu.VMEM((rows, d), BF16)],
        compiler_params=_params(2),
        name="fourier_seq3",
    )(t2, x.reshape(bt, c_n, b_n, ah_n, SUBLANES, d), mod, *pq3, *tw2, cc, sc, w["b_w_out"], w["b_b_out"])
    return out.reshape(bt, s, d)


def _tile_rows(s, target):
    tm = min(s, target)
    while s % tm:
        tm //= 2
    return tm


def _trunk(x, mods, w, *, mlp_tm=1024, conv_tm=1024, ff_chunk=1024):
    depth = w["norm_mix"].shape[0]
    s = x.shape[1]
    for i in range(depth):
        kind, j = i % N_MIXERS, i // N_MIXERS
        if kind == 0:
            x = _sconv_layer(x, mods[i], w, i, j, tm=_tile_rows(s, conv_tm))
        elif kind == 1:
            x = _fourier_layer(x, mods[i], w, i, j)
        else:
            x = _conformer_layer(x, mods[i], w, i, j, tm=_tile_rows(s, conv_tm))
        x = _mlp_layer(x, mods[i], w, i, final=(i == depth - 1), tm=_tile_rows(s, mlp_tm),
                       ff_chunk=ff_chunk)
    return x


def kernel(x_prompt, x_sample, c_prompt, c_sample, ada_w, ada_b, norm_mix, norm_mlp, a_w_in, a_conv_w, a_w_out, b_w_out, b_b_out, c_w_pw1, c_b_pw1, c_dw_w, c_dw_b, c_ln_g, c_ln_b, c_w_pw2, c_b_pw2, mlp_w_up, mlp_w_down, final_norm):
    d = x_prompt.shape[-1]
    n_p, n_s = c_prompt.shape[0], c_sample.shape[0]
    pad_rows = -(n_p + n_s) % SUBLANES
    c_all = jnp.concatenate([c_prompt, c_sample, jnp.zeros((pad_rows, d), F32)], axis=0)
    mod = _ada_modulation(c_all, ada_w, ada_b)
    mods_p = mod[:, :n_p, None, :]
    mods_s = mod[:, n_p:n_p + n_s, None, :]
    w = dict(
        norm_mix=_rows(norm_mix), norm_mlp=_rows(norm_mlp), final_norm=final_norm.reshape(1, d),
        a_w_in=a_w_in.astype(BF16), a_conv_w=a_conv_w, a_w_out=a_w_out.astype(BF16),
        b_w_out=b_w_out.astype(BF16), b_b_out=_rows(b_b_out),
        c_w_pw1=c_w_pw1.astype(BF16), c_b_pw1=_rows(c_b_pw1), c_dw_w=c_dw_w, c_dw_b=_rows(c_dw_b),
        c_ln_g=_rows(c_ln_g), c_ln_b=_rows(c_ln_b), c_w_pw2=c_w_pw2.astype(BF16), c_b_pw2=_rows(c_b_pw2),
        mlp_f32=(mlp_w_up, mlp_w_down), mlp_bf16=[None] * mlp_w_up.shape[0],
    )
    return _trunk(x_prompt, mods_p, w), _trunk(x_sample, mods_s, w)
```
